```python
import math
import jax, jax.numpy as jnp
from jax import lax
import numpy as np

D_MODEL = 2048
BATCH = 8
SEQ = 2048
DEPTH = 1

CHUNK = 64
Q_BLOCK = 128
MIX_WIDTH = D_MODEL
DIFF_HEADS = 4
DIFF_QK_DIM = 128
DIFF_V_DIM = 2 * DIFF_QK_DIM
DIFF_WIDTH = DIFF_HEADS * DIFF_V_DIM
GMLP_CHUNK = 128
GMLP_GROUPS = 8
GMLP_GROUP_DIM = 128
GMLP_WIDTH = GMLP_GROUPS * GMLP_GROUP_DIM
IN_DIM = 3 * DIFF_HEADS * 2 * DIFF_QK_DIM // 2 * 2 // 2 * 1 + 0
IN_DIM = 2 * (DIFF_HEADS * 2 * DIFF_QK_DIM) + DIFF_WIDTH + 2 * GMLP_WIDTH
NUM_BUCKETS = 32
MAX_DISTANCE = 128
FFN_HIDDEN = ((8 * D_MODEL // 3 + 255) // 256) * 256
PLE_DIM = 256
EPS = 1e-6
NEG_INF = -1e30

kernel_name = "hybrid_diffattn_gmlp_chunk_causal_block"


def rms_norm(x, g):
    x32 = x.astype(jnp.float32)
    y = x32 * lax.rsqrt(jnp.mean(x32 * x32, axis=-1, keepdims=True) + EPS)
    return (y * g.astype(jnp.float32)).astype(x.dtype)


def layer_norm(x, g, b):
    x32 = x.astype(jnp.float32)
    mu = jnp.mean(x32, axis=-1, keepdims=True)
    xc = x32 - mu
    y = xc * lax.rsqrt(jnp.mean(xc * xc, axis=-1, keepdims=True) + EPS)
    return (y * g.astype(jnp.float32) + b.astype(jnp.float32)).astype(x.dtype)


def t5_bucket(rel):
    half = NUM_BUCKETS // 2
    max_exact = half // 2
    ret = jnp.where(rel > 0, half, 0)
    n = jnp.abs(rel)
    nf = jnp.maximum(n, 1).astype(jnp.float32)
    large = max_exact + (jnp.log(nf / max_exact) / math.log(MAX_DISTANCE / max_exact)
                         * (half - max_exact)).astype(jnp.int32)
    large = jnp.minimum(large, half - 1)
    return ret + jnp.where(n < max_exact, n, large)


def diff_attention(q, k, v, lam, rel_bias):
    seq = q.shape[1]
    scale = DIFF_QK_DIM ** -0.5
    outs = []
    for blk in range(seq // Q_BLOCK):
        q0, q1 = blk * Q_BLOCK, (blk + 1) * Q_BLOCK
        qb, kb, vb = q[:, q0:q1], k[:, :q1], v[:, :q1]
        qpos = jnp.arange(q0, q1, dtype=jnp.int32)
        kpos = jnp.arange(q1, dtype=jnp.int32)
        logits = jnp.einsum('bqhmd,bkhmd->bhmqk', qb, kb).astype(jnp.float32) * scale
        bias = rel_bias[t5_bucket(kpos[None, :] - qpos[:, None])]
        bias = jnp.transpose(bias, (2, 0, 1)).astype(jnp.float32)
        mask = (kpos[None, :] // CHUNK) <= (qpos[:, None] // CHUNK)
        logits = jnp.where(mask, logits + bias[None, :, None], NEG_INF)
        probs = jax.nn.softmax(logits, axis=-1)
        attn = probs[:, :, 0] - lam * probs[:, :, 1]
        outs.append(jnp.einsum('bhqk,bkhe->bqhe', attn.astype(v.dtype), vb))
    return jnp.concatenate(outs, axis=1)


def gmlp_spatial_gate(u, vg, ln_g, ln_b, ws, bs):
    bsz, seq, _ = u.shape
    nb = seq // GMLP_CHUNK
    vn = layer_norm(vg, ln_g, ln_b).reshape(bsz, nb, GMLP_CHUNK, GMLP_GROUPS, GMLP_GROUP_DIM)
    pos = jnp.arange(GMLP_CHUNK)
    mask = (pos[None, :] // CHUNK) <= (pos[:, None] // CHUNK)
    ws_m = jnp.where(mask[None], ws, jnp.zeros_like(ws))
    s = jnp.einsum('gij,bnjgc->bnigc', ws_m, vn) + jnp.transpose(bs)[None, None, :, :, None]
    out = u.reshape(bsz, nb, GMLP_CHUNK, GMLP_GROUPS, GMLP_GROUP_DIM) * s
    return out.reshape(bsz, seq, GMLP_WIDTH)


def setup_inputs(seed: int = 0) -> dict:
    key = jax.random.key(seed)
    ks = jax.random.split(key, 24)
    f32 = jnp.float32
    nrm = lambda k, shape, s: jax.random.normal(k, shape, f32) * s
    gain = lambda k, shape: 1.0 + 0.05 * jax.random.normal(k, shape, f32)
    return {
        "x": jax.random.normal(ks[0], (BATCH, SEQ, D_MODEL), f32),
        "p": jax.random.normal(ks[1], (DEPTH, BATCH, SEQ, PLE_DIM), f32),
        "w_in": nrm(ks[2], (DEPTH, D_MODEL, IN_DIM), D_MODEL ** -0.5),
        "w_out": nrm(ks[3], (DEPTH, MIX_WIDTH, D_MODEL), MIX_WIDTH ** -0.5),
        "attn_norm_g": gain(ks[4], (DEPTH, D_MODEL)),
        "ffn_norm_g": gain(ks[5], (DEPTH, D_MODEL)),
        "final_norm_g": gain(ks[6], (D_MODEL,)),
        "lambda_q1": nrm(ks[7], (DEPTH, DIFF_QK_DIM), 0.1),
        "lambda_k1": nrm(ks[8], (DEPTH, DIFF_QK_DIM), 0.1),
        "lambda_q2": nrm(ks[9], (DEPTH, DIFF_QK_DIM), 0.1),
        "lambda_k2": nrm(ks[10], (DEPTH, DIFF_QK_DIM), 0.1),
        "subln_g": gain(ks[11], (DEPTH, DIFF_V_DIM)),
        "rel_bias": nrm(ks[12], (NUM_BUCKETS, DIFF_HEADS), 0.2),
        "gmlp_ln_g": gain(ks[13], (DEPTH, GMLP_WIDTH)),
        "gmlp_ln_b": nrm(ks[14], (DEPTH, GMLP_WIDTH), 0.02),
        "gmlp_ws": nrm(ks[15], (DEPTH, GMLP_GROUPS, GMLP_CHUNK, GMLP_CHUNK), GMLP_CHUNK ** -0.5),
        "gmlp_b": gain(ks[16], (DEPTH, GMLP_GROUPS, GMLP_CHUNK)),
        "ffn_w1": nrm(ks[17], (DEPTH, D_MODEL, FFN_HIDDEN), D_MODEL ** -0.5),
        "ffn_w3": nrm(ks[18], (DEPTH, D_MODEL, FFN_HIDDEN), D_MODEL ** -0.5),
        "ffn_w2": nrm(ks[19], (DEPTH, FFN_HIDDEN, D_MODEL), FFN_HIDDEN ** -0.5),
        "pl_w_up": nrm(ks[20], (DEPTH, PLE_DIM, D_MODEL), PLE_DIM ** -0.5),
        "pl_w_gate": nrm(ks[21], (DEPTH, D_MODEL, D_MODEL), D_MODEL ** -0.5),
    }


def reference(x, p, w_in, w_out, attn_norm_g, ffn_norm_g, final_norm_g,
              lambda_q1, lambda_k1, lambda_q2, lambda_k2, subln_g, rel_bias,
              gmlp_ln_g, gmlp_ln_b, gmlp_ws, gmlp_b, ffn_w1, ffn_w3, ffn_w2,
              pl_w_up, pl_w_gate):
    bsz, seq, _ = x.shape
    qk_cols = DIFF_HEADS * 2 * DIFF_QK_DIM
    splits = [qk_cols, 2 * qk_cols, 2 * qk_cols + DIFF_WIDTH,
              2 * qk_cols + DIFF_WIDTH + GMLP_WIDTH]
    h = x
    for i in range(DEPTH):
        a = rms_norm(h, attn_norm_g[i])
        z = a @ w_in[i]
        zq, zk, zv, zu, zg = jnp.split(z, splits, axis=-1)
        q = zq.reshape(bsz, seq, DIFF_HEADS, 2, DIFF_QK_DIM)
        k = zk.reshape(bsz, seq, DIFF_HEADS, 2, DIFF_QK_DIM)
        v = zv.reshape(bsz, seq, DIFF_HEADS, DIFF_V_DIM)
        lam_init = 0.8 - 0.6 * math.exp(-0.3 * i)
        lam = (jnp.exp(jnp.sum(lambda_q1[i].astype(jnp.float32) * lambda_k1[i].astype(jnp.float32)))
               - jnp.exp(jnp.sum(lambda_q2[i].astype(jnp.float32) * lambda_k2[i].astype(jnp.float32)))
               + lam_init)
        o_diff = diff_attention(q, k, v, lam, rel_bias)
        o_diff = rms_norm(o_diff, subln_g[i]) * (1.0 - lam_init)
        o_diff = o_diff.reshape(bsz, seq, DIFF_WIDTH)
        o_gmlp = gmlp_spatial_gate(jax.nn.gelu(zu), jax.nn.gelu(zg),
                                   gmlp_ln_g[i], gmlp_ln_b[i], gmlp_ws[i], gmlp_b[i])
        mixed = jnp.concatenate([o_diff.astype(h.dtype), o_gmlp.astype(h.dtype)], axis=-1)
        h = h + mixed @ w_out[i]
        f = rms_norm(h, ffn_norm_g[i])
        h = h + (jax.nn.silu(f @ ffn_w1[i]) * (f @ ffn_w3[i])) @ ffn_w2[i]
        h = h + (p[i] @ pl_w_up[i]) * jax.nn.sigmoid(h @ pl_w_gate[i])
    return rms_norm(h, final_norm_g)
```

```python
import functools
import math

import numpy as np
import jax
import jax.numpy as jnp
from jax import lax
from jax.experimental import pallas as pl
from jax.experimental.pallas import tpu as pltpu

F32 = jnp.float32
BF16 = jnp.bfloat16

CHUNK = 64
DIFF_HEADS = 4
DIFF_QK_DIM = 128
DIFF_V_DIM = 2 * DIFF_QK_DIM
DIFF_WIDTH = DIFF_HEADS * DIFF_V_DIM
GMLP_CHUNK = 128
GMLP_GROUPS = 8
GMLP_GROUP_DIM = 128
GMLP_WIDTH = GMLP_GROUPS * GMLP_GROUP_DIM
NUM_BUCKETS = 32
MAX_DISTANCE = 128
EPS = 1e-6
NEG_INF = -1e30
ATTN_SCALE = DIFF_QK_DIM ** -0.5

LANES = 128
VMEM_LIMIT_BYTES = 56 * 1024 * 1024

IN_TM, IN_TN = 1024, 1024
ATTN_TQ = 256
MIX_TM = 512
FFN_TM, FFN_TH = 512, 512
PLE_TM = 512

_NT_DIMS = (((1,), (1,)), ((), ()))


def _dot(a, b):
    return jnp.dot(a, b, preferred_element_type=F32)


def _rms_scale(x):
    return x * lax.rsqrt(jnp.mean(x * x, axis=-1, keepdims=True) + EPS)


def _lane_tile(x, n):
    return x if n == 1 else jnp.concatenate([x] * n, axis=1)


def _compiler_params(semantics):
    return pltpu.CompilerParams(dimension_semantics=semantics,
                                vmem_limit_bytes=VMEM_LIMIT_BYTES)


def _in_proj_kernel(x_ref, g_ref, w_ref, z_ref, a_sc, *, gelu_from):
    j = pl.program_id(1)

    @pl.when(j == 0)
    def _():
        a_sc[...] = (_rms_scale(x_ref[...]) * g_ref[...]).astype(BF16)

    z = _dot(a_sc[...], w_ref[...])

    @pl.when(j < gelu_from)
    def _():
        z_ref[...] = z.astype(BF16)

    @pl.when(j >= gelu_from)
    def _():
        z_ref[...] = jax.nn.gelu(z).astype(BF16)


def _in_proj(x2d, g, w_bf16, gelu_from_col):
    m, d = x2d.shape
    n = w_bf16.shape[1]
    assert m % IN_TM == 0 and n % IN_TN == 0 and gelu_from_col % IN_TN == 0
    return pl.pallas_call(
        functools.partial(_in_proj_kernel, gelu_from=gelu_from_col // IN_TN),
        grid=(m // IN_TM, n // IN_TN),
        in_specs=[
            pl.BlockSpec((IN_TM, d), lambda i, j: (i, 0)),
            pl.BlockSpec((1, d), lambda i, j: (0, 0)),
            pl.BlockSpec((d, IN_TN), lambda i, j: (0, j)),
        ],
        out_specs=pl.BlockSpec((IN_TM, IN_TN), lambda i, j: (i, j)),
        out_shape=jax.ShapeDtypeStruct((m, n), BF16),
        scratch_shapes=[pltpu.VMEM((IN_TM, d), BF16)],
        compiler_params=_compiler_params(("arbitrary", "arbitrary")),
        name="in_proj",
    )(x2d, g, w_bf16)


def _t5_bucket_np(rel):
    half = NUM_BUCKETS // 2
    max_exact = half // 2
    ret = np.where(rel > 0, half, 0)
    n = np.abs(rel)
    nf = np.maximum(n, 1).astype(np.float32)
    large = max_exact + (np.log(nf / np.float32(max_exact))
                         / np.float32(math.log(MAX_DISTANCE / max_exact))
                         * np.float32(half - max_exact)).astype(np.int32)
    large = np.minimum(large, half - 1)
    return (ret + np.where(n < max_exact, n, large)).astype(np.int32)


def _near_bucket_table(seq):
    i = np.arange(ATTN_TQ)[:, None]
    j = np.arange(2 * ATTN_TQ)[None, :] - ATTN_TQ
    visible = np.floor_divide(j, CHUNK) <= i // CHUNK
    table = np.where(visible, _t5_bucket_np(j - i), -1).astype(np.int32)
    far = np.unique(_t5_bucket_np(-np.arange(ATTN_TQ + 1, max(seq, ATTN_TQ + 2))))
    assert far.size == 1, "far keys must share one bias bucket"
    return table, int(far[0])


def _bias_tile_kernel(bucket_ref, rb_ref, out_ref, *, buckets, far_bucket):
    bucket = bucket_ref[...]
    for h in range(DIFF_HEADS):
        shift = rb_ref[far_bucket, h]
        tile = jnp.full(bucket.shape, NEG_INF, F32)
        for b in buckets:
            tile = jnp.where(bucket == b, rb_ref[b, h] - shift, tile)
        out_ref[h] = tile


def _bias_tile(rel_bias, seq):
    table, far_bucket = _near_bucket_table(seq)
    buckets = tuple(int(b) for b in np.unique(table) if b >= 0)
    return pl.pallas_call(
        functools.partial(_bias_tile_kernel, buckets=buckets, far_bucket=far_bucket),
        in_specs=[pl.BlockSpec(memory_space=pltpu.VMEM),
                  pl.BlockSpec(memory_space=pltpu.SMEM)],
        out_specs=pl.BlockSpec(memory_space=pltpu.VMEM),
        out_shape=jax.ShapeDtypeStruct((DIFF_HEADS,) + table.shape, F32),
        name="bias_tile",
    )(jnp.asarray(table), rel_bias)


def _attn_kernel(q_ref, k_ref, v_ref, bias_ref, lq1_ref, lk1_ref, lq2_ref, lk2_ref,
                 g_ref, o_ref, m_sc, l_sc, acc_sc, *, lam_init):
    tq = ATTN_TQ
    qi = pl.program_id(2)
    q = q_ref[...]
    q_parts = (q[:, :DIFF_QK_DIM], q[:, DIFF_QK_DIM:])

    def tile_step(k_start, bias, init):
        k = k_ref[pl.ds(k_start, tq), :]
        v = v_ref[pl.ds(k_start, tq), :]
        for m in range(2):
            km = k[:, m * DIFF_QK_DIM:(m + 1) * DIFF_QK_DIM]
            s = lax.dot_general(q_parts[m], km, _NT_DIMS,
                                preferred_element_type=F32) * ATTN_SCALE
            if bias is not None:
                s = s + bias
            m_cur = jnp.max(s, axis=1, keepdims=True)
            if init:
                m_next = jnp.broadcast_to(m_cur, (tq, LANES))
            else:
                m_prev = m_sc[m]
                m_next = jnp.maximum(m_prev, m_cur)
            p = jnp.exp(s - _lane_tile(m_next, tq // LANES))
            p_sum = jnp.sum(p, axis=1, keepdims=True)
            pv = _dot(p.astype(BF16), v)
            if init:
                l_sc[m] = jnp.broadcast_to(p_sum, (tq, LANES))
                acc_sc[m] = pv
            else:
                alpha = jnp.exp(m_prev - m_next)
                l_sc[m] = alpha * l_sc[m] + p_sum
                acc_sc[m] = acc_sc[m] * _lane_tile(alpha, DIFF_V_DIM // LANES) + pv
            m_sc[m] = m_next

    q_start = pl.multiple_of(qi * tq, tq)
    tile_step(q_start, bias_ref[:, tq:], init=True)

    @pl.when(qi > 0)
    def _():
        tile_step(pl.multiple_of(q_start - tq, tq), bias_ref[:, :tq], init=False)

    def far_body(t, carry):
        tile_step(pl.multiple_of(t * tq, tq), None, init=False)
        return carry

    lax.fori_loop(0, qi - 1, far_body, 0)

    lam = (jnp.exp(jnp.sum(lq1_ref[...] * lk1_ref[...], axis=-1, keepdims=True))
           - jnp.exp(jnp.sum(lq2_ref[...] * lk2_ref[...], axis=-1, keepdims=True))
           + lam_init)
    n_rep = DIFF_V_DIM // LANES
    o = (acc_sc[0] / _lane_tile(l_sc[0], n_rep)
         - lam * (acc_sc[1] / _lane_tile(l_sc[1], n_rep)))
    o_ref[...] = (_rms_scale(o) * g_ref[...] * (1.0 - lam_init)).astype(BF16)


def _attention(z, bias_tile, lq1, lk1, lq2, lk2, subln_g, bsz, seq, lam_init):
    tq = ATTN_TQ
    assert seq % tq == 0
    nq = seq // tq
    h = DIFF_HEADS
    vec = pl.BlockSpec((1, DIFF_QK_DIM), lambda b, hh, qi: (0, 0))
    return pl.pallas_call(
        functools.partial(_attn_kernel, lam_init=lam_init),
        grid=(bsz, h, nq),
        in_specs=[
            pl.BlockSpec((tq, DIFF_V_DIM), lambda b, hh, qi: (b * nq + qi, hh)),
            pl.BlockSpec((seq, DIFF_V_DIM), lambda b, hh, qi: (b, h + hh)),
            pl.BlockSpec((seq, DIFF_V_DIM), lambda b, hh, qi: (b, 2 * h + hh)),
            pl.BlockSpec((None, tq, 2 * tq), lambda b, hh, qi: (hh, 0, 0)),
            vec, vec, vec, vec,
            pl.BlockSpec((1, DIFF_V_DIM), lambda b, hh, qi: (0, 0)),
        ],
        out_specs=pl.BlockSpec((tq, DIFF_V_DIM), lambda b, hh, qi: (b * nq + qi, hh)),
        out_shape=jax.ShapeDtypeStruct((bsz * seq, DIFF_WIDTH), BF16),
        scratch_shapes=[
            pltpu.VMEM((2, tq, LANES), F32),
            pltpu.VMEM((2, tq, LANES), F32),
            pltpu.VMEM((2, tq, DIFF_V_DIM), F32),
        ],
        compiler_params=_compiler_params(("arbitrary", "arbitrary", "arbitrary")),
        name="diff_attention",
    )(z, z, z, bias_tile, lq1, lk1, lq2, lk2, subln_g)


def _mix_out_kernel(od_ref, gu_ref, gv_ref, x_ref, lng_ref, lnb_ref, ws_ref, bs_ref,
                    wo_ref, fg_ref, h1_ref, f_ref, gm_sc):
    tm = od_ref.shape[0]
    gv = gv_ref[...].astype(F32)
    mu = jnp.mean(gv, axis=-1, keepdims=True)
    xc = gv - mu
    vn = (xc * lax.rsqrt(jnp.mean(xc * xc, axis=-1, keepdims=True) + EPS)
          * lng_ref[...] + lnb_ref[...]).astype(BF16)

    row = lax.broadcasted_iota(jnp.int32, (GMLP_CHUNK, GMLP_CHUNK), 0)
    col = lax.broadcasted_iota(jnp.int32, (GMLP_CHUNK, GMLP_CHUNK), 1)
    causal = (col // CHUNK) <= (row // CHUNK)
    for g in range(GMLP_GROUPS):
        ws_g = jnp.where(causal, ws_ref[g], 0.0).astype(BF16)
        cols = slice(g * GMLP_GROUP_DIM, (g + 1) * GMLP_GROUP_DIM)
        for r in range(tm // GMLP_CHUNK):
            rows = slice(r * GMLP_CHUNK, (r + 1) * GMLP_CHUNK)
            s = _dot(ws_g, vn[rows, cols]) + bs_ref[g]
            gm_sc[rows, cols] = (gu_ref[rows, cols].astype(F32) * s).astype(BF16)

    h1 = (x_ref[...] + _dot(od_ref[...], wo_ref[:DIFF_WIDTH, :])
          + _dot(gm_sc[...], wo_ref[DIFF_WIDTH:, :]))
    h1_ref[...] = h1
    f_ref[...] = (_rms_scale(h1) * fg_ref[...]).astype(BF16)


def _mix_out(o_diff, z, x2d, ln_g, ln_b, ws, bs, w_out_bf16, ffn_g):
    m, d = x2d.shape
    tm = MIX_TM
    assert m % tm == 0 and tm % GMLP_CHUNK == 0
    u_blk = (2 * DIFF_WIDTH + DIFF_WIDTH) // GMLP_WIDTH
    full = lambda shape: pl.BlockSpec(shape, lambda i: (0,) * len(shape))
    return pl.pallas_call(
        _mix_out_kernel,
        grid=(m // tm,),
        in_specs=[
            pl.BlockSpec((tm, DIFF_WIDTH), lambda i: (i, 0)),
            pl.BlockSpec((tm, GMLP_WIDTH), lambda i: (i, u_blk)),
            pl.BlockSpec((tm, GMLP_WIDTH), lambda i: (i, u_blk + 1)),
            pl.BlockSpec((tm, d), lambda i: (i, 0)),
            full((1, GMLP_WIDTH)), full((1, GMLP_WIDTH)),
            full((GMLP_GROUPS, GMLP_CHUNK, GMLP_CHUNK)),
            full((GMLP_GROUPS, GMLP_CHUNK, 1)),
            full((DIFF_WIDTH + GMLP_WIDTH, d)),
            full((1, d)),
        ],
        out_specs=[pl.BlockSpec((tm, d), lambda i: (i, 0)),
                   pl.BlockSpec((tm, d), lambda i: (i, 0))],
        out_shape=[jax.ShapeDtypeStruct((m, d), F32),
                   jax.ShapeDtypeStruct((m, d), BF16)],
        scratch_shapes=[pltpu.VMEM((tm, GMLP_WIDTH), BF16)],
        compiler_params=_compiler_params(("arbitrary",)),
        name="mix_out",
    )(o_diff, z, z, x2d, ln_g, ln_b, ws, bs, w_out_bf16, ffn_g)


def _ffn_kernel(f_ref, h1_ref, w1_ref, w3_ref, w2_ref, o_ref):
    k = pl.program_id(1)
    f = f_ref[...]
    act = (jax.nn.silu(_dot(f, w1_ref[...])) * _dot(f, w3_ref[...])).astype(BF16)
    contrib = _dot(act, w2_ref[...])

    @pl.when(k == 0)
    def _():
        o_ref[...] = h1_ref[...] + contrib

    @pl.when(k > 0)
    def _():
        o_ref[...] += contrib


def _ffn(f, h1, w1, w3, w2):
    m, d = h1.shape
    hid = w1.shape[1]
    tm, th = FFN_TM, FFN_TH
    assert m % tm == 0 and hid % th == 0
    return pl.pallas_call(
        _ffn_kernel,
        grid=(m // tm, hid // th),
        in_specs=[
            pl.BlockSpec((tm, d), lambda i, k: (i, 0)),
            pl.BlockSpec((tm, d), lambda i, k: (i, 0)),
            pl.BlockSpec((d, th), lambda i, k: (0, k)),
            pl.BlockSpec((d, th), lambda i, k: (0, k)),
            pl.BlockSpec((th, d), lambda i, k: (k, 0)),
        ],
        out_specs=pl.BlockSpec((tm, d), lambda i, k: (i, 0)),
        out_shape=jax.ShapeDtypeStruct((m, d), F32),
        compiler_params=_compiler_params(("arbitrary", "arbitrary")),
        name="ffn",
    )(f, h1, w1, w3, w2)


def _ple_kernel(h2_ref, p_ref, wg_ref, wu_ref, g_ref, o_ref, *, final_norm):
    h2 = h2_ref[...]
    gate = _dot(h2.astype(BF16), wg_ref[...])
    up = _dot(p_ref[...].astype(BF16), wu_ref[...])
    h3 = h2 + up * jax.nn.sigmoid(gate)
    if final_norm:
        h3 = _rms_scale(h3) * g_ref[...]
    o_ref[...] = h3


def _ple(h2, p2d, w_gate, w_up, final_g, final_norm):
    m, d = h2.shape
    pd = p2d.shape[1]
    tm = PLE_TM
    assert m % tm == 0
    full = lambda shape: pl.BlockSpec(shape, lambda i: (0,) * len(shape))
    return pl.pallas_call(
        functools.partial(_ple_kernel, final_norm=final_norm),
        grid=(m // tm,),
        in_specs=[
            pl.BlockSpec((tm, d), lambda i: (i, 0)),
            pl.BlockSpec((tm, pd), lambda i: (i, 0)),
            full((d, d)), full((pd, d)), full((1, d)),
        ],
        out_specs=pl.BlockSpec((tm, d), lambda i: (i, 0)),
        out_shape=jax.ShapeDtypeStruct((m, d), F32),
        compiler_params=_compiler_params(("arbitrary",)),
        name="ple",
    )(h2, p2d, w_gate, w_up, final_g)


def kernel(x, p, w_in, w_out, attn_norm_g, ffn_norm_g, final_norm_g, lambda_q1, lambda_k1,
           lambda_q2, lambda_k2, subln_g, rel_bias, gmlp_ln_g, gmlp_ln_b, gmlp_ws, gmlp_b,
           ffn_w1, ffn_w3, ffn_w2, pl_w_up, pl_w_gate):
    bsz, seq, d = x.shape
    depth = w_in.shape[0]
    qk_cols = DIFF_HEADS * 2 * DIFF_QK_DIM
    assert w_in.shape[2] == 2 * qk_cols + DIFF_WIDTH + 2 * GMLP_WIDTH
    assert seq % GMLP_CHUNK == 0
    row = lambda v: v.reshape(1, -1)

    bias_tile = _bias_tile(rel_bias, seq)
    h = x.reshape(bsz * seq, d)
    for i in range(depth):
        lam_init = 0.8 - 0.6 * math.exp(-0.3 * i)
        z = _in_proj(h, row(attn_norm_g[i]), w_in[i].astype(BF16),
                     gelu_from_col=2 * qk_cols + DIFF_WIDTH)
        o_diff = _attention(z, bias_tile, row(lambda_q1[i]), row(lambda_k1[i]),
                            row(lambda_q2[i]), row(lambda_k2[i]), row(subln_g[i]),
                            bsz, seq, lam_init)
        h1, f = _mix_out(o_diff, z, h, row(gmlp_ln_g[i]), row(gmlp_ln_b[i]), gmlp_ws[i],
                         gmlp_b[i].reshape(GMLP_GROUPS, GMLP_CHUNK, 1),
                         w_out[i].astype(BF16), row(ffn_norm_g[i]))
        h2 = _ffn(f, h1, ffn_w1[i].astype(BF16), ffn_w3[i].astype(BF16),
                  ffn_w2[i].astype(BF16))
        h = _ple(h2, p[i].reshape(bsz * seq, -1), pl_w_gate[i].astype(BF16),
                 pl_w_up[i].astype(BF16), row(final_norm_g), final_norm=(i == depth - 1))
    return h.reshape(bsz, seq, d)
```

```python
import functools
import math

import numpy as np
import jax
import jax.numpy as jnp
from jax import lax
from jax.experimental import pallas as pl
from jax.experimental.pallas import tpu as pltpu

F32 = jnp.float32
BF16 = jnp.bfloat16

CHUNK = 64
DIFF_HEADS = 4
DIFF_QK_DIM = 128
DIFF_V_DIM = 2 * DIFF_QK_DIM
DIFF_WIDTH = DIFF_HEADS * DIFF_V_DIM
GMLP_CHUNK = 128
GMLP_GROUPS = 8
GMLP_GROUP_DIM = 128
GMLP_WIDTH = GMLP_GROUPS * GMLP_GROUP_DIM
NUM_BUCKETS = 32
MAX_DISTANCE = 128
EPS = 1e-6
NEG_INF = -1e30
ATTN_SCALE = DIFF_QK_DIM ** -0.5

LANES = 128
VMEM_LIMIT_BYTES = 56 * 1024 * 1024

IN_TM, IN_TN = 1024, 1024
ATTN_TQ = 256
MIX_TM = 512
FFN_TM, FFN_TH = 512, 512
PLE_TM = 512

_NT_DIMS = (((1,), (1,)), ((), ()))


def _dot(a, b):
    return jnp.dot(a, b, preferred_element_type=F32)


def _rms_scale(x):
    return x * lax.rsqrt(jnp.mean(x * x, axis=-1, keepdims=True) + EPS)


def _lane_tile(x, n):
    return x if n == 1 else jnp.concatenate([x] * n, axis=1)


def _compiler_params(semantics):
    return pltpu.CompilerParams(dimension_semantics=semantics,
                                vmem_limit_bytes=VMEM_LIMIT_BYTES)


def _in_proj_kernel(x_ref, g_ref, w_ref, z_ref, a_sc, *, gelu_from):
    j = pl.program_id(1)

    @pl.when(j == 0)
    def _():
        a_sc[...] = (_rms_scale(x_ref[...]) * g_ref[...]).astype(BF16)

    @pl.when(j < gelu_from)
    def _():
        z_ref[...] = _dot(a_sc[...], w_ref[...]).astype(BF16)

    @pl.when(j >= gelu_from)
    def _():
        z_ref[...] = jax.nn.gelu(_dot(a_sc[...], w_ref[...])).astype(BF16)


def _in_proj(x2d, g, w_bf16, gelu_from_col):
    m, d = x2d.shape
    n = w_bf16.shape[1]
    assert m % IN_TM == 0 and n % IN_TN == 0 and gelu_from_col % IN_TN == 0
    return pl.pallas_call(
        functools.partial(_in_proj_kernel, gelu_from=gelu_from_col // IN_TN),
        grid=(m // IN_TM, n // IN_TN),
        in_specs=[
            pl.BlockSpec((IN_TM, d), lambda i, j: (i, 0)),
            pl.BlockSpec((1, d), lambda i, j: (0, 0)),
            pl.BlockSpec((d, IN_TN), lambda i, j: (0, j)),
        ],
        out_specs=pl.BlockSpec((IN_TM, IN_TN), lambda i, j: (i, j)),
        out_shape=jax.ShapeDtypeStruct((m, n), BF16),
        scratch_shapes=[pltpu.VMEM((IN_TM, d), BF16)],
        compiler_params=_compiler_params(("arbitrary", "arbitrary")),
        name="in_proj",
    )(x2d, g, w_bf16)


def _t5_bucket_np(rel):
    half = NUM_BUCKETS // 2
    max_exact = half // 2
    ret = np.where(rel > 0, half, 0)
    n = np.abs(rel)
    nf = np.maximum(n, 1).astype(np.float32)
    large = max_exact + (np.log(nf / np.float32(max_exact))
                         / np.float32(math.log(MAX_DISTANCE / max_exact))
                         * np.float32(half - max_exact)).astype(np.int32)
    large = np.minimum(large, half - 1)
    return (ret + np.where(n < max_exact, n, large)).astype(np.int32)


def _near_bucket_table(seq):
    i = np.arange(ATTN_TQ)[:, None]
    j = np.arange(2 * ATTN_TQ)[None, :] - ATTN_TQ
    visible = np.floor_divide(j, CHUNK) <= i // CHUNK
    table = np.where(visible, _t5_bucket_np(j - i), -1).astype(np.int32)
    far = np.unique(_t5_bucket_np(-np.arange(ATTN_TQ + 1, max(seq, ATTN_TQ + 2))))
    assert far.size == 1, "far keys must share one bias bucket"
    return table, int(far[0])


def _bias_tile_kernel(bucket_ref, rb_ref, out_ref, *, buckets, far_bucket):
    bucket = bucket_ref[...]
    for h in range(DIFF_HEADS):
        shift = rb_ref[far_bucket, h]
        tile = jnp.full(bucket.shape, NEG_INF, F32)
        for b in buckets:
            tile = jnp.where(bucket == b, rb_ref[b, h] - shift, tile)
        out_ref[h] = tile


def _bias_tile(rel_bias, seq):
    table, far_bucket = _near_bucket_table(seq)
    buckets = tuple(int(b) for b in np.unique(table) if b >= 0)
    return pl.pallas_call(
        functools.partial(_bias_tile_kernel, buckets=buckets, far_bucket=far_bucket),
        in_specs=[pl.BlockSpec(memory_space=pltpu.VMEM),
                  pl.BlockSpec(memory_space=pltpu.SMEM)],
        out_specs=pl.BlockSpec(memory_space=pltpu.VMEM),
        out_shape=jax.ShapeDtypeStruct((DIFF_HEADS,) + table.shape, F32),
        name="bias_tile",
    )(jnp.asarray(table), rel_bias)


def _attn_kernel(q_ref, k_ref, v_ref, bias_ref, lq1_ref, lk1_ref, lq2_ref, lk2_ref,
                 g_ref, o_ref, m_sc, l_sc, acc_sc, *, lam_init):
    tq = ATTN_TQ
    qi = pl.program_id(1)

    def tile_step(k_start, bias_cols, init):
        for h in range(DIFF_HEADS):
            v = v_ref[pl.ds(k_start, tq), h * DIFF_V_DIM:(h + 1) * DIFF_V_DIM]
            for m in range(2):
                c = 2 * h + m
                cols = slice(c * DIFF_QK_DIM, (c + 1) * DIFF_QK_DIM)
                s = lax.dot_general(q_ref[:, cols], k_ref[pl.ds(k_start, tq), cols], _NT_DIMS,
                                    preferred_element_type=F32) * ATTN_SCALE
                if bias_cols is not None:
                    s = s + bias_ref[h, :, bias_cols]
                m_cur = jnp.max(s, axis=1, keepdims=True)
                if init:
                    m_next = jnp.broadcast_to(m_cur, (tq, LANES))
                else:
                    m_prev = m_sc[c]
                    m_next = jnp.maximum(m_prev, m_cur)
                p = jnp.exp(s - _lane_tile(m_next, tq // LANES))
                p_sum = jnp.sum(p, axis=1, keepdims=True)
                pv = _dot(p.astype(BF16), v)
                if init:
                    l_sc[c] = jnp.broadcast_to(p_sum, (tq, LANES))
                    acc_sc[c] = pv
                else:
                    alpha = jnp.exp(m_prev - m_next)
                    l_sc[c] = alpha * l_sc[c] + p_sum
                    acc_sc[c] = acc_sc[c] * _lane_tile(alpha, DIFF_V_DIM // LANES) + pv
                m_sc[c] = m_next

    q_start = pl.multiple_of(qi * tq, tq)
    tile_step(q_start, slice(tq, 2 * tq), init=True)

    @pl.when(qi > 0)
    def _():
        tile_step(pl.multiple_of(q_start - tq, tq), slice(0, tq), init=False)

    def far_body(t, carry):
        tile_step(pl.multiple_of(t * tq, tq), None, init=False)
        return carry

    lax.fori_loop(0, qi - 1, far_body, 0)

    lam = (jnp.exp(jnp.sum(lq1_ref[...] * lk1_ref[...], axis=-1, keepdims=True))
           - jnp.exp(jnp.sum(lq2_ref[...] * lk2_ref[...], axis=-1, keepdims=True))
           + lam_init)
    n_rep = DIFF_V_DIM // LANES
    for h in range(DIFF_HEADS):
        o = (acc_sc[2 * h] / _lane_tile(l_sc[2 * h], n_rep)
             - lam * (acc_sc[2 * h + 1] / _lane_tile(l_sc[2 * h + 1], n_rep)))
        o_ref[:, h * DIFF_V_DIM:(h + 1) * DIFF_V_DIM] = (
            _rms_scale(o) * g_ref[...] * (1.0 - lam_init)).astype(BF16)


def _attention(z, bias_tile, lq1, lk1, lq2, lk2, subln_g, bsz, seq, lam_init):
    tq = ATTN_TQ
    assert seq % tq == 0
    nq = seq // tq
    n_chains = 2 * DIFF_HEADS
    vec = pl.BlockSpec((1, DIFF_QK_DIM), lambda b, qi: (0, 0))
    return pl.pallas_call(
        functools.partial(_attn_kernel, lam_init=lam_init),
        grid=(bsz, nq),
        in_specs=[
            pl.BlockSpec((tq, DIFF_WIDTH), lambda b, qi: (b * nq + qi, 0)),
            pl.BlockSpec((seq, DIFF_WIDTH), lambda b, qi: (b, 1)),
            pl.BlockSpec((seq, DIFF_WIDTH), lambda b, qi: (b, 2)),
            pl.BlockSpec((DIFF_HEADS, tq, 2 * tq), lambda b, qi: (0, 0, 0)),
            vec, vec, vec, vec,
            pl.BlockSpec((1, DIFF_V_DIM), lambda b, qi: (0, 0)),
        ],
        out_specs=pl.BlockSpec((tq, DIFF_WIDTH), lambda b, qi: (b * nq + qi, 0)),
        out_shape=jax.ShapeDtypeStruct((bsz * seq, DIFF_WIDTH), BF16),
        scratch_shapes=[
            pltpu.VMEM((n_chains, tq, LANES), F32),
            pltpu.VMEM((n_chains, tq, LANES), F32),
            pltpu.VMEM((n_chains, tq, DIFF_V_DIM), F32),
        ],
        compiler_params=_compiler_params(("arbitrary", "arbitrary")),
        name="diff_attention",
    )(z, z, z, bias_tile, lq1, lk1, lq2, lk2, subln_g)


def _mix_out_kernel(od_ref, gu_ref, gv_ref, x_ref, lng_ref, lnb_ref, ws_ref, bs_ref,
                    wo_ref, fg_ref, h1_ref, f_ref, gm_sc):
    tm = od_ref.shape[0]
    gv = gv_ref[...].astype(F32)
    mu = jnp.mean(gv, axis=-1, keepdims=True)
    xc = gv - mu
    vn = (xc * lax.rsqrt(jnp.mean(xc * xc, axis=-1, keepdims=True) + EPS)
          * lng_ref[...] + lnb_ref[...]).astype(BF16)

    row = lax.broadcasted_iota(jnp.int32, (GMLP_CHUNK, GMLP_CHUNK), 0)
    col = lax.broadcasted_iota(jnp.int32, (GMLP_CHUNK, GMLP_CHUNK), 1)
    causal = (col // CHUNK) <= (row // CHUNK)
    for g in range(GMLP_GROUPS):
        ws_g = jnp.where(causal, ws_ref[g], 0.0).astype(BF16)
        cols = slice(g * GMLP_GROUP_DIM, (g + 1) * GMLP_GROUP_DIM)
        for r in range(tm // GMLP_CHUNK):
            rows = slice(r * GMLP_CHUNK, (r + 1) * GMLP_CHUNK)
            s = _dot(ws_g, vn[rows, cols]) + bs_ref[g]
            gm_sc[rows, cols] = (gu_ref[rows, cols].astype(F32) * s).astype(BF16)

    h1 = (x_ref[...] + _dot(od_ref[...], wo_ref[:DIFF_WIDTH, :])
          + _dot(gm_sc[...], wo_ref[DIFF_WIDTH:, :]))
    h1_ref[...] = h1
    f_ref[...] = (_rms_scale(h1) * fg_ref[...]).astype(BF16)


def _mix_out(o_diff, z, x2d, ln_g, ln_b, ws, bs, w_out_bf16, ffn_g):
    m, d = x2d.shape
    tm = MIX_TM
    assert m % tm == 0 and tm % GMLP_CHUNK == 0
    u_blk = (2 * DIFF_WIDTH + DIFF_WIDTH) // GMLP_WIDTH
    full = lambda shape: pl.BlockSpec(shape, lambda i: (0,) * len(shape))
    return pl.pallas_call(
        _mix_out_kernel,
        grid=(m // tm,),
        in_specs=[
            pl.BlockSpec((tm, DIFF_WIDTH), lambda i: (i, 0)),
            pl.BlockSpec((tm, GMLP_WIDTH), lambda i: (i, u_blk)),
            pl.BlockSpec((tm, GMLP_WIDTH), lambda i: (i, u_blk + 1)),
            pl.BlockSpec((tm, d), lambda i: (i, 0)),
            full((1, GMLP_WIDTH)), full((1, GMLP_WIDTH)),
            full((GMLP_GROUPS, GMLP_CHUNK, GMLP_CHUNK)),
            full((GMLP_GROUPS, GMLP_CHUNK, 1)),
            full((DIFF_WIDTH + GMLP_WIDTH, d)),
            full((1, d)),
        ],
        out_specs=[pl.BlockSpec((tm, d), lambda i: (i, 0)),
                   pl.BlockSpec((tm, d), lambda i: (i, 0))],
        out_shape=[jax.ShapeDtypeStruct((m, d), F32),
                   jax.ShapeDtypeStruct((m, d), BF16)],
        scratch_shapes=[pltpu.VMEM((tm, GMLP_WIDTH), BF16)],
        compiler_params=_compiler_params(("arbitrary",)),
        name="mix_out",
    )(o_diff, z, z, x2d, ln_g, ln_b, ws, bs, w_out_bf16, ffn_g)


def _ffn_kernel(f_ref, h1_ref, w1_ref, w3_ref, w2_ref, o_ref):
    k = pl.program_id(1)
    f = f_ref[...]
    act = (jax.nn.silu(_dot(f, w1_ref[...])) * _dot(f, w3_ref[...])).astype(BF16)
    base = jnp.where(k == 0, h1_ref[...], o_ref[...])
    o_ref[...] = base + _dot(act, w2_ref[...])


def _ffn(f, h1, w1, w3, w2):
    m, d = h1.shape
    hid = w1.shape[1]
    tm, th = FFN_TM, FFN_TH
    assert m % tm == 0 and hid % th == 0
    return pl.pallas_call(
        _ffn_kernel,
        grid=(m // tm, hid // th),
        in_specs=[
            pl.BlockSpec((tm, d), lambda i, k: (i, 0)),
            pl.BlockSpec((tm, d), lambda i, k: (i, 0)),
            pl.BlockSpec((d, th), lambda i, k: (0, k)),
            pl.BlockSpec((d, th), lambda i, k: (0, k)),
            pl.BlockSpec((th, d), lambda i, k: (k, 0)),
        ],
        out_specs=pl.BlockSpec((tm, d), lambda i, k: (i, 0)),
        out_shape=jax.ShapeDtypeStruct((m, d), F32),
        compiler_params=_compiler_params(("arbitrary", "arbitrary")),
        name="ffn",
    )(f, h1, w1, w3, w2)


def _ple_kernel(h2_ref, p_ref, wg_ref, wu_ref, g_ref, o_ref, *, final_norm):
    h2 = h2_ref[...]
    gate = _dot(h2.astype(BF16), wg_ref[...])
    up = _dot(p_ref[...].astype(BF16), wu_ref[...])
    h3 = h2 + up * jax.nn.sigmoid(gate)
    if final_norm:
        h3 = _rms_scale(h3) * g_ref[...]
    o_ref[...] = h3


def _ple(h2, p2d, w_gate, w_up, final_g, final_norm):
    m, d = h2.shape
    pd = p2d.shape[1]
    tm = PLE_TM
    assert m % tm == 0
    full = lambda shape: pl.BlockSpec(shape, lambda i: (0,) * len(shape))
    return pl.pallas_call(
        functools.partial(_ple_kernel, final_norm=final_norm),
        grid=(m // tm,),
        in_specs=[
            pl.BlockSpec((tm, d), lambda i: (i, 0)),
            pl.BlockSpec((tm, pd), lambda i: (i, 0)),
            full((d, d)), full((pd, d)), full((1, d)),
        ],
        out_specs=pl.BlockSpec((tm, d), lambda i: (i, 0)),
        out_shape=jax.ShapeDtypeStruct((m, d), F32),
        compiler_params=_compiler_params(("arbitrary",)),
        name="ple",
    )(h2, p2d, w_gate, w_up, final_g)


def kernel(x, p, w_in, w_out, attn_norm_g, ffn_norm_g, final_norm_g, lambda_q1, lambda_k1,
           lambda_q2, lambda_k2, subln_g, rel_bias, gmlp_ln_g, gmlp_ln_b, gmlp_ws, gmlp_b,
           ffn_w1, ffn_w3, ffn_w2, pl_w_up, pl_w_gate):
    bsz, seq, d = x.shape
    depth = w_in.shape[0]
    qk_cols = DIFF_HEADS * 2 * DIFF_QK_DIM
    assert w_in.shape[2] == 2 * qk_cols + DIFF_WIDTH + 2 * GMLP_WIDTH
    assert seq % GMLP_CHUNK == 0
    row = lambda v: v.reshape(1, -1)

    bias_tile = _bias_tile(rel_bias, seq)
    h = x.reshape(bsz * seq, d)
    for i in range(depth):
        lam_init = 0.8 - 0.6 * math.exp(-0.3 * i)
        z = _in_proj(h, row(attn_norm_g[i]), w_in[i].astype(BF16),
                     gelu_from_col=2 * qk_cols + DIFF_WIDTH)
        o_diff = _attention(z, bias_tile, row(lambda_q1[i]), row(lambda_k1[i]),
                            row(lambda_q2[i]), row(lambda_k2[i]), row(subln_g[i]),
                            bsz, seq, lam_init)
        h1, f = _mix_out(o_diff, z, h, row(gmlp_ln_g[i]), row(gmlp_ln_b[i]), gmlp_ws[i],
                         gmlp_b[i].reshape(GMLP_GROUPS, GMLP_CHUNK, 1),
                         w_out[i].astype(BF16), row(ffn_norm_g[i]))
        h2 = _ffn(f, h1, ffn_w1[i].astype(BF16), ffn_w3[i].astype(BF16),
                  ffn_w2[i].astype(BF16))
        h = _ple(h2, p[i].reshape(bsz * seq, -1), pl_w_gate[i].astype(BF16),
                 pl_w_up[i].astype(BF16), row(final_norm_g), final_norm=(i == depth - 1))
    return h.reshape(bsz, seq, d)
```

```python
import functools
import math

import numpy as np
import jax
import jax.numpy as jnp
from jax import lax
from jax.experimental import pallas as pl
from jax.experimental.pallas import tpu as pltpu

F32 = jnp.float32
BF16 = jnp.bfloat16

CHUNK = 64
DIFF_HEADS = 4
DIFF_QK_DIM = 128
DIFF_V_DIM = 2 * DIFF_QK_DIM
DIFF_WIDTH = DIFF_HEADS * DIFF_V_DIM
GMLP_CHUNK = 128
GMLP_GROUPS = 8
GMLP_GROUP_DIM = 128
GMLP_WIDTH = GMLP_GROUPS * GMLP_GROUP_DIM
NUM_BUCKETS = 32
MAX_DISTANCE = 128
EPS = 1e-6
NEG_INF = -1e30
ATTN_SCALE = DIFF_QK_DIM ** -0.5

LANES = 128
VMEM_LIMIT_BYTES = 56 * 1024 * 1024

IN_TM, IN_TN = 512, 1024
ATTN_TQ = 256
MIX_TM = 512
FFN_TM, FFN_TH = 1024, 512
PLE_TM = 512

_NT_DIMS = (((1,), (1,)), ((), ()))


def _dot(a, b):
    return jnp.dot(a, b, preferred_element_type=F32)


def _rms_scale(x):
    return x * lax.rsqrt(jnp.mean(x * x, axis=-1, keepdims=True) + EPS)


def _lane_tile(x, n):
    return x if n == 1 else jnp.concatenate([x] * n, axis=1)


def _compiler_params(semantics):
    return pltpu.CompilerParams(dimension_semantics=semantics,
                                vmem_limit_bytes=VMEM_LIMIT_BYTES)


def _in_proj_kernel(x_ref, g_ref, w_ref, z_ref, *, gelu_from_col):
    a = (_rms_scale(x_ref[...]) * g_ref[...]).astype(BF16)
    for c in range(0, w_ref.shape[1], IN_TN):
        zc = _dot(a, w_ref[:, c:c + IN_TN])
        if c >= gelu_from_col:
            zc = jax.nn.gelu(zc)
        z_ref[:, c:c + IN_TN] = zc.astype(BF16)


def _in_proj(x2d, g, w_bf16, gelu_from_col):
    m, d = x2d.shape
    n = w_bf16.shape[1]
    assert m % IN_TM == 0 and n % IN_TN == 0 and gelu_from_col % IN_TN == 0
    return pl.pallas_call(
        functools.partial(_in_proj_kernel, gelu_from_col=gelu_from_col),
        grid=(m // IN_TM,),
        in_specs=[
            pl.BlockSpec((IN_TM, d), lambda i: (i, 0)),
            pl.BlockSpec((1, d), lambda i: (0, 0)),
            pl.BlockSpec((d, n), lambda i: (0, 0), pipeline_mode=pl.Buffered(1)),
        ],
        out_specs=pl.BlockSpec((IN_TM, n), lambda i: (i, 0)),
        out_shape=jax.ShapeDtypeStruct((m, n), BF16),
        compiler_params=_compiler_params(("arbitrary",)),
        name="in_proj",
    )(x2d, g, w_bf16)


def _t5_bucket_np(rel):
    half = NUM_BUCKETS // 2
    max_exact = half // 2
    ret = np.where(rel > 0, half, 0)
    n = np.abs(rel)
    nf = np.maximum(n, 1).astype(np.float32)
    large = max_exact + (np.log(nf / np.float32(max_exact))
                         / np.float32(math.log(MAX_DISTANCE / max_exact))
                         * np.float32(half - max_exact)).astype(np.int32)
    large = np.minimum(large, half - 1)
    return (ret + np.where(n < max_exact, n, large)).astype(np.int32)


def _near_bucket_table(seq):
    i = np.arange(ATTN_TQ)[:, None]
    j = np.arange(2 * ATTN_TQ)[None, :] - ATTN_TQ
    visible = np.floor_divide(j, CHUNK) <= i // CHUNK
    table = np.where(visible, _t5_bucket_np(j - i), -1).astype(np.int32)
    far = np.unique(_t5_bucket_np(-np.arange(ATTN_TQ + 1, max(seq, ATTN_TQ + 2))))
    assert far.size == 1, "far keys must share one bias bucket"
    return table, int(far[0])


def _bias_tile_kernel(bucket_ref, rb_ref, out_ref, *, buckets, far_bucket):
    bucket = bucket_ref[...]
    for h in range(DIFF_HEADS):
        shift = rb_ref[far_bucket, h]
        tile = jnp.full(bucket.shape, NEG_INF, F32)
        for b in buckets:
            tile = jnp.where(bucket == b, rb_ref[b, h] - shift, tile)
        out_ref[h] = tile


def _bias_tile(rel_bias, seq):
    table, far_bucket = _near_bucket_table(seq)
    buckets = tuple(int(b) for b in np.unique(table) if b >= 0)
    return pl.pallas_call(
        functools.partial(_bias_tile_kernel, buckets=buckets, far_bucket=far_bucket),
        in_specs=[pl.BlockSpec(memory_space=pltpu.VMEM),
                  pl.BlockSpec(memory_space=pltpu.SMEM)],
        out_specs=pl.BlockSpec(memory_space=pltpu.VMEM),
        out_shape=jax.ShapeDtypeStruct((DIFF_HEADS,) + table.shape, F32),
        name="bias_tile",
    )(jnp.asarray(table), rel_bias)


def _attn_kernel(q_ref, k_ref, v_ref, bias_ref, lq1_ref, lk1_ref, lq2_ref, lk2_ref,
                 g_ref, o_ref, m_sc, l_sc, acc_sc, *, lam_init):
    tq = ATTN_TQ
    qi = pl.program_id(1)

    def tile_step(k_start, bias_cols, init):
        for h in range(DIFF_HEADS):
            v = v_ref[pl.ds(k_start, tq), h * DIFF_V_DIM:(h + 1) * DIFF_V_DIM]
            for m in range(2):
                c = 2 * h + m
                cols = slice(c * DIFF_QK_DIM, (c + 1) * DIFF_QK_DIM)
                s = lax.dot_general(q_ref[:, cols], k_ref[pl.ds(k_start, tq), cols], _NT_DIMS,
                                    preferred_element_type=F32) * ATTN_SCALE
                if bias_cols is not None:
                    s = s + bias_ref[h, :, bias_cols]
                m_cur = jnp.max(s, axis=1, keepdims=True)
                if init:
                    m_next = jnp.broadcast_to(m_cur, (tq, LANES))
                else:
                    m_prev = m_sc[c]
                    m_next = jnp.maximum(m_prev, m_cur)
                p = jnp.exp(s - _lane_tile(m_next, tq // LANES))
                p_sum = jnp.sum(p, axis=1, keepdims=True)
                pv = _dot(p.astype(BF16), v)
                if init:
                    l_sc[c] = jnp.broadcast_to(p_sum, (tq, LANES))
                    acc_sc[c] = pv
                else:
                    alpha = jnp.exp(m_prev - m_next)
                    l_sc[c] = alpha * l_sc[c] + p_sum
                    acc_sc[c] = acc_sc[c] * _lane_tile(alpha, DIFF_V_DIM // LANES) + pv
                m_sc[c] = m_next

    q_start = pl.multiple_of(qi * tq, tq)
    tile_step(q_start, slice(tq, 2 * tq), init=True)

    @pl.when(qi > 0)
    def _():
        tile_step(pl.multiple_of(q_start - tq, tq), slice(0, tq), init=False)

    def far_body(t, carry):
        tile_step(pl.multiple_of(t * tq, tq), None, init=False)
        return carry

    lax.fori_loop(0, qi - 1, far_body, 0)

    lam = (jnp.exp(jnp.sum(lq1_ref[...] * lk1_ref[...], axis=-1, keepdims=True))
           - jnp.exp(jnp.sum(lq2_ref[...] * lk2_ref[...], axis=-1, keepdims=True))
           + lam_init)
    n_rep = DIFF_V_DIM // LANES
    for h in range(DIFF_HEADS):
        o = (acc_sc[2 * h] / _lane_tile(l_sc[2 * h], n_rep)
             - lam * (acc_sc[2 * h + 1] / _lane_tile(l_sc[2 * h + 1], n_rep)))
        o_ref[:, h * DIFF_V_DIM:(h + 1) * DIFF_V_DIM] = (
            _rms_scale(o) * g_ref[...] * (1.0 - lam_init)).astype(BF16)


def _attention(z, bias_tile, lq1, lk1, lq2, lk2, subln_g, bsz, seq, lam_init):
    tq = ATTN_TQ
    assert seq % tq == 0
    nq = seq // tq
    n_chains = 2 * DIFF_HEADS
    vec = pl.BlockSpec((1, DIFF_QK_DIM), lambda b, qi: (0, 0))
    return pl.pallas_call(
        functools.partial(_attn_kernel, lam_init=lam_init),
        grid=(bsz, nq),
        in_specs=[
            pl.BlockSpec((tq, DIFF_WIDTH), lambda b, qi: (b * nq + qi, 0)),
            pl.BlockSpec((seq, DIFF_WIDTH), lambda b, qi: (b, 1)),
            pl.BlockSpec((seq, DIFF_WIDTH), lambda b, qi: (b, 2)),
            pl.BlockSpec((DIFF_HEADS, tq, 2 * tq), lambda b, qi: (0, 0, 0)),
            vec, vec, vec, vec,
            pl.BlockSpec((1, DIFF_V_DIM), lambda b, qi: (0, 0)),
        ],
        out_specs=pl.BlockSpec((tq, DIFF_WIDTH), lambda b, qi: (b * nq + qi, 0)),
        out_shape=jax.ShapeDtypeStruct((bsz * seq, DIFF_WIDTH), BF16),
        scratch_shapes=[
            pltpu.VMEM((n_chains, tq, LANES), F32),
            pltpu.VMEM((n_chains, tq, LANES), F32),
            pltpu.VMEM((n_chains, tq, DIFF_V_DIM), F32),
        ],
        compiler_params=_compiler_params(("arbitrary", "arbitrary")),
        name="diff_attention",
    )(z, z, z, bias_tile, lq1, lk1, lq2, lk2, subln_g)


def _mix_out_kernel(od_ref, gu_ref, gv_ref, x_ref, lng_ref, lnb_ref, ws_ref, bs_ref,
                    wo_ref, h1_ref, gm_sc):
    tm = od_ref.shape[0]
    gv = gv_ref[...].astype(F32)
    mu = jnp.mean(gv, axis=-1, keepdims=True)
    xc = gv - mu
    vn = (xc * lax.rsqrt(jnp.mean(xc * xc, axis=-1, keepdims=True) + EPS)
          * lng_ref[...] + lnb_ref[...]).astype(BF16)

    row = lax.broadcasted_iota(jnp.int32, (GMLP_CHUNK, GMLP_CHUNK), 0)
    col = lax.broadcasted_iota(jnp.int32, (GMLP_CHUNK, GMLP_CHUNK), 1)
    causal = (col // CHUNK) <= (row // CHUNK)
    for g in range(GMLP_GROUPS):
        ws_g = jnp.where(causal, ws_ref[g], 0.0).astype(BF16)
        cols = slice(g * GMLP_GROUP_DIM, (g + 1) * GMLP_GROUP_DIM)
        for r in range(tm // GMLP_CHUNK):
            rows = slice(r * GMLP_CHUNK, (r + 1) * GMLP_CHUNK)
            s = _dot(ws_g, vn[rows, cols]) + bs_ref[g]
            gm_sc[rows, cols] = (gu_ref[rows, cols].astype(F32) * s).astype(BF16)

    h1_ref[...] = (x_ref[...] + _dot(od_ref[...], wo_ref[:DIFF_WIDTH, :])
                   + _dot(gm_sc[...], wo_ref[DIFF_WIDTH:, :]))


def _mix_out(o_diff, z, x2d, ln_g, ln_b, ws, bs, w_out_bf16):
    m, d = x2d.shape
    tm = MIX_TM
    assert m % tm == 0 and tm % GMLP_CHUNK == 0
    u_blk = (2 * DIFF_WIDTH + DIFF_WIDTH) // GMLP_WIDTH
    full = lambda shape: pl.BlockSpec(shape, lambda i: (0,) * len(shape))
    return pl.pallas_call(
        _mix_out_kernel,
        grid=(m // tm,),
        in_specs=[
            pl.BlockSpec((tm, DIFF_WIDTH), lambda i: (i, 0)),
            pl.BlockSpec((tm, GMLP_WIDTH), lambda i: (i, u_blk)),
            pl.BlockSpec((tm, GMLP_WIDTH), lambda i: (i, u_blk + 1)),
            pl.BlockSpec((tm, d), lambda i: (i, 0)),
            full((1, GMLP_WIDTH)), full((1, GMLP_WIDTH)),
            full((GMLP_GROUPS, GMLP_CHUNK, GMLP_CHUNK)),
            full((GMLP_GROUPS, GMLP_CHUNK, 1)),
            full((DIFF_WIDTH + GMLP_WIDTH, d)),
        ],
        out_specs=pl.BlockSpec((tm, d), lambda i: (i, 0)),
        out_shape=jax.ShapeDtypeStruct((m, d), F32),
        scratch_shapes=[pltpu.VMEM((tm, GMLP_WIDTH), BF16)],
        compiler_params=_compiler_params(("arbitrary",)),
        name="mix_out",
    )(o_diff, z, z, x2d, ln_g, ln_b, ws, bs, w_out_bf16)


def _ffn_kernel(h1_ref, fg_ref, w1_ref, w3_ref, w2_ref, o_ref, f_sc):
    k = pl.program_id(1)

    def hidden_tile(f, base):
        act = (jax.nn.silu(_dot(f, w1_ref[...])) * _dot(f, w3_ref[...])).astype(BF16)
        o_ref[...] = base + _dot(act, w2_ref[...])

    @pl.when(k == 0)
    def _():
        h1 = h1_ref[...]
        f = (_rms_scale(h1) * fg_ref[...]).astype(BF16)
        f_sc[...] = f
        hidden_tile(f, h1)

    @pl.when(k > 0)
    def _():
        hidden_tile(f_sc[...], o_ref[...])


def _tile_major(w, tn):
    k, n = w.shape
    return w.reshape(k, n // tn, tn).transpose(1, 0, 2)


def _ffn(h1, ffn_g, w1_tiles, w3_tiles, w2):
    m, d = h1.shape
    n_tiles, _, th = w1_tiles.shape
    tm = FFN_TM
    assert m % tm == 0 and w2.shape == (n_tiles * th, d)
    return pl.pallas_call(
        _ffn_kernel,
        grid=(m // tm, n_tiles),
        in_specs=[
            pl.BlockSpec((tm, d), lambda i, k: (i, 0)),
            pl.BlockSpec((1, d), lambda i, k: (0, 0)),
            pl.BlockSpec((None, d, th), lambda i, k: (k, 0, 0)),
            pl.BlockSpec((None, d, th), lambda i, k: (k, 0, 0)),
            pl.BlockSpec((th, d), lambda i, k: (k, 0)),
        ],
        out_specs=pl.BlockSpec((tm, d), lambda i, k: (i, 0)),
        out_shape=jax.ShapeDtypeStruct((m, d), F32),
        scratch_shapes=[pltpu.VMEM((tm, d), BF16)],
        compiler_params=_compiler_params(("arbitrary", "arbitrary")),
        name="ffn",
    )(h1, ffn_g, w1_tiles, w3_tiles, w2)


def _ple_kernel(h2_ref, p_ref, wg_ref, wu_ref, g_ref, o_ref, *, final_norm):
    h2 = h2_ref[...]
    gate = _dot(h2.astype(BF16), wg_ref[...])
    up = _dot(p_ref[...].astype(BF16), wu_ref[...])
    h3 = h2 + up * jax.nn.sigmoid(gate)
    if final_norm:
        h3 = _rms_scale(h3) * g_ref[...]
    o_ref[...] = h3


def _ple(h2, p2d, w_gate, w_up, final_g, final_norm):
    m, d = h2.shape
    pd = p2d.shape[1]
    tm = PLE_TM
    assert m % tm == 0
    full = lambda shape: pl.BlockSpec(shape, lambda i: (0,) * len(shape))
    return pl.pallas_call(
        functools.partial(_ple_kernel, final_norm=final_norm),
        grid=(m // tm,),
        in_specs=[
            pl.BlockSpec((tm, d), lambda i: (i, 0)),
            pl.BlockSpec((tm, pd), lambda i: (i, 0)),
            full((d, d)), full((pd, d)), full((1, d)),
        ],
        out_specs=pl.BlockSpec((tm, d), lambda i: (i, 0)),
        out_shape=jax.ShapeDtypeStruct((m, d), F32),
        compiler_params=_compiler_params(("arbitrary",)),
        name="ple",
    )(h2, p2d, w_gate, w_up, final_g)


def kernel(x, p, w_in, w_out, attn_norm_g, ffn_norm_g, final_norm_g, lambda_q1, lambda_k1,
           lambda_q2, lambda_k2, subln_g, rel_bias, gmlp_ln_g, gmlp_ln_b, gmlp_ws, gmlp_b,
           ffn_w1, ffn_w3, ffn_w2, pl_w_up, pl_w_gate):
    bsz, seq, d = x.shape
    depth = w_in.shape[0]
    qk_cols = DIFF_HEADS * 2 * DIFF_QK_DIM
    assert w_in.shape[2] == 2 * qk_cols + DIFF_WIDTH + 2 * GMLP_WIDTH
    assert seq % GMLP_CHUNK == 0
    row = lambda v: v.reshape(1, -1)

    bias_tile = _bias_tile(rel_bias, seq)
    h = x.reshape(bsz * seq, d)
    for i in range(depth):
        lam_init = 0.8 - 0.6 * math.exp(-0.3 * i)
        z = _in_proj(h, row(attn_norm_g[i]), w_in[i].astype(BF16),
                     gelu_from_col=2 * qk_cols + DIFF_WIDTH)
        o_diff = _attention(z, bias_tile, row(lambda_q1[i]), row(lambda_k1[i]),
                            row(lambda_q2[i]), row(lambda_k2[i]), row(subln_g[i]),
                            bsz, seq, lam_init)
        h1 = _mix_out(o_diff, z, h, row(gmlp_ln_g[i]), row(gmlp_ln_b[i]), gmlp_ws[i],
                      gmlp_b[i].reshape(GMLP_GROUPS, GMLP_CHUNK, 1), w_out[i].astype(BF16))
        h2 = _ffn(h1, row(ffn_norm_g[i]), _tile_major(ffn_w1[i].astype(BF16), FFN_TH),
                  _tile_major(ffn_w3[i].astype(BF16), FFN_TH), ffn_w2[i].astype(BF16))
        h = _ple(h2, p[i].reshape(bsz * seq, -1), pl_w_gate[i].astype(BF16),
                 pl_w_up[i].astype(BF16), row(final_norm_g), final_norm=(i == depth - 1))
    return h.reshape(bsz, seq, d)
```

```python
import functools
import math

import numpy as np
import jax
import jax.numpy as jnp
from jax import lax
from jax.experimental import pallas as pl
from jax.experimental.pallas import tpu as pltpu

F32 = jnp.float32
BF16 = jnp.bfloat16

CHUNK = 64
DIFF_HEADS = 4
DIFF_QK_DIM = 128
DIFF_V_DIM = 2 * DIFF_QK_DIM
DIFF_WIDTH = DIFF_HEADS * DIFF_V_DIM
GMLP_CHUNK = 128
GMLP_GROUPS = 8
GMLP_GROUP_DIM = 128
GMLP_WIDTH = GMLP_GROUPS * GMLP_GROUP_DIM
NUM_BUCKETS = 32
MAX_DISTANCE = 128
EPS = 1e-6
NEG_INF = -1e30
ATTN_SCALE = DIFF_QK_DIM ** -0.5

VMEM_LIMIT_BYTES = 56 * 1024 * 1024

IN_TM, IN_TN = 512, 1024
ATTN_TQ = 256
MIX_TM = 512
FFN_TM, FFN_TH = 1024, 512
PLE_TM = 512

_NT_DIMS = (((1,), (1,)), ((), ()))


def _dot(a, b):
    return jnp.dot(a, b, preferred_element_type=F32)


def _rms_scale(x):
    return x * lax.rsqrt(jnp.mean(x * x, axis=-1, keepdims=True) + EPS)


def _compiler_params(semantics):
    return pltpu.CompilerParams(dimension_semantics=semantics,
                                vmem_limit_bytes=VMEM_LIMIT_BYTES)


def _in_proj_kernel(x_ref, g_ref, wqk_ref, wvt_ref, wug_ref, qk_ref, vt_ref, ug_ref):
    a = (_rms_scale(x_ref[...]) * g_ref[...]).astype(BF16)
    for c in range(0, wqk_ref.shape[1], IN_TN):
        qk_ref[:, c:c + IN_TN] = _dot(a, wqk_ref[:, c:c + IN_TN]).astype(BF16)
    vt = lax.dot_general(wvt_ref[...], a, _NT_DIMS, preferred_element_type=F32).astype(BF16)
    for t in range(vt_ref.shape[0]):
        vt_ref[t] = vt[:, t * ATTN_TQ:(t + 1) * ATTN_TQ]
    for c in range(0, wug_ref.shape[1], IN_TN):
        ug_ref[:, c:c + IN_TN] = jax.nn.gelu(_dot(a, wug_ref[:, c:c + IN_TN])).astype(BF16)


def _in_proj(x2d, g, w_qk, w_vt, w_ug, bsz, seq):
    m, d = x2d.shape
    tm = IN_TM
    assert seq % tm == 0 and tm % ATTN_TQ == 0
    assert w_qk.shape[1] % IN_TN == 0 and w_ug.shape[1] % IN_TN == 0
    tiles_per_seq = seq // tm
    kt = tm // ATTN_TQ
    resident = lambda shape: pl.BlockSpec(shape, lambda i: (0, 0), pipeline_mode=pl.Buffered(1))
    return pl.pallas_call(
        _in_proj_kernel,
        grid=(m // tm,),
        in_specs=[
            pl.BlockSpec((tm, d), lambda i: (i, 0)),
            pl.BlockSpec((1, d), lambda i: (0, 0)),
            resident(w_qk.shape), resident(w_vt.shape), resident(w_ug.shape),
        ],
        out_specs=[
            pl.BlockSpec((tm, w_qk.shape[1]), lambda i: (i, 0)),
            pl.BlockSpec((None, kt, DIFF_WIDTH, ATTN_TQ),
                         lambda i: (i // tiles_per_seq, i % tiles_per_seq, 0, 0)),
            pl.BlockSpec((tm, w_ug.shape[1]), lambda i: (i, 0)),
        ],
        out_shape=[
            jax.ShapeDtypeStruct((m, w_qk.shape[1]), BF16),
            jax.ShapeDtypeStruct((bsz, seq // ATTN_TQ, DIFF_WIDTH, ATTN_TQ), BF16),
            jax.ShapeDtypeStruct((m, w_ug.shape[1]), BF16),
        ],
        compiler_params=_compiler_params(("arbitrary",)),
        name="in_proj",
    )(x2d, g, w_qk, w_vt, w_ug)


def _t5_bucket_np(rel):
    half = NUM_BUCKETS // 2
    max_exact = half // 2
    ret = np.where(rel > 0, half, 0)
    n = np.abs(rel)
    nf = np.maximum(n, 1).astype(np.float32)
    large = max_exact + (np.log(nf / np.float32(max_exact))
                         / np.float32(math.log(MAX_DISTANCE / max_exact))
                         * np.float32(half - max_exact)).astype(np.int32)
    large = np.minimum(large, half - 1)
    return (ret + np.where(n < max_exact, n, large)).astype(np.int32)


def _near_bucket_table(seq):
    i = np.arange(ATTN_TQ)[None, :]
    j = np.arange(2 * ATTN_TQ)[:, None] - ATTN_TQ
    visible = np.floor_divide(j, CHUNK) <= i // CHUNK
    table = np.where(visible, _t5_bucket_np(j - i), -1).astype(np.int32)
    far = np.unique(_t5_bucket_np(-np.arange(ATTN_TQ + 1, max(seq, ATTN_TQ + 2))))
    assert far.size == 1, "far keys must share one bias bucket"
    return table, int(far[0])


def _bias_tile_kernel(bucket_ref, rb_ref, out_ref, *, buckets, far_bucket):
    bucket = bucket_ref[...]
    for h in range(DIFF_HEADS):
        shift = rb_ref[far_bucket, h]
        tile = jnp.full(bucket.shape, NEG_INF / ATTN_SCALE, F32)
        for b in buckets:
            tile = jnp.where(bucket == b, (rb_ref[b, h] - shift) / ATTN_SCALE, tile)
        out_ref[h] = tile


def _bias_tile(rel_bias, seq):
    table, far_bucket = _near_bucket_table(seq)
    buckets = tuple(int(b) for b in np.unique(table) if b >= 0)
    return pl.pallas_call(
        functools.partial(_bias_tile_kernel, buckets=buckets, far_bucket=far_bucket),
        in_specs=[pl.BlockSpec(memory_space=pltpu.VMEM),
                  pl.BlockSpec(memory_space=pltpu.SMEM)],
        out_specs=pl.BlockSpec(memory_space=pltpu.VMEM),
        out_shape=jax.ShapeDtypeStruct((DIFF_HEADS,) + table.shape, F32),
        name="bias_tile",
    )(jnp.asarray(table), rel_bias)


def _attn_kernel(q_ref, k_ref, vt_ref, bias_ref, lq1_ref, lk1_ref, lq2_ref, lk2_ref,
                 g_ref, o_ref, m_sc, l_sc, alpha_sc, p_sc, acc_sc, *, lam_init):
    tq = ATTN_TQ
    qi = pl.program_id(1)
    exp2_scale = ATTN_SCALE * math.log2(math.e)
    chains = [(h, m) for h in range(DIFF_HEADS) for m in range(2)]

    def softmax_tile(kt, bias_rows, init):
        k_start = pl.multiple_of(kt * tq, tq)
        for c, (h, _) in enumerate(chains):
            cols = slice(c * DIFF_QK_DIM, (c + 1) * DIFF_QK_DIM)
            s = lax.dot_general(k_ref[pl.ds(k_start, tq), cols], q_ref[:, cols], _NT_DIMS,
                                preferred_element_type=F32)
            if bias_rows is not None:
                s = s + bias_ref[h, bias_rows, :]
            m_cur = jnp.max(s, axis=0, keepdims=True)
            if init:
                m_next = m_cur
            else:
                m_prev = m_sc[c]
                m_next = jnp.maximum(m_prev, m_cur)
            p = jnp.exp2((s - m_next) * exp2_scale)
            p_sum = jnp.sum(p, axis=0, keepdims=True)
            if init:
                l_sc[c] = p_sum
            else:
                alpha = jnp.exp2((m_prev - m_next) * exp2_scale)
                alpha_sc[c] = alpha
                l_sc[c] = alpha * l_sc[c] + p_sum
            m_sc[c] = m_next
            p_sc[c] = p.astype(BF16)

    def value_tile(kt, first):
        for c, (h, _) in enumerate(chains):
            pv = _dot(vt_ref[kt, h * DIFF_V_DIM:(h + 1) * DIFF_V_DIM, :], p_sc[c])
            acc_sc[c] = pv if first else acc_sc[c] * alpha_sc[c] + pv

    softmax_tile(qi, slice(tq, 2 * tq), init=True)

    @pl.when(qi == 0)
    def _():
        value_tile(qi, first=True)

    @pl.when(qi > 0)
    def _():
        value_tile(qi, first=True)
        softmax_tile(qi - 1, slice(0, tq), init=False)

    def far_body(step, carry):
        value_tile(qi - step + 1, first=False)
        softmax_tile(qi - step, None, init=False)
        return carry

    lax.fori_loop(2, qi + 1, far_body, 0)

    @pl.when(qi > 0)
    def _():
        value_tile(0, first=False)

    lam = (jnp.exp(jnp.sum(lq1_ref[...] * lk1_ref[...], axis=-1, keepdims=True))
           - jnp.exp(jnp.sum(lq2_ref[...] * lk2_ref[...], axis=-1, keepdims=True))
           + lam_init)
    for h in range(DIFF_HEADS):
        o = acc_sc[2 * h] / l_sc[2 * h] - lam * (acc_sc[2 * h + 1] / l_sc[2 * h + 1])
        y = (o * lax.rsqrt(jnp.mean(o * o, axis=0, keepdims=True) + EPS)
             * g_ref[...] * (1.0 - lam_init))
        o_ref[:, h * DIFF_V_DIM:(h + 1) * DIFF_V_DIM] = y.T.astype(BF16)


def _attention(zqk, vt, bias_tile, lq1, lk1, lq2, lk2, subln_g_col, bsz, seq, lam_init):
    tq = ATTN_TQ
    assert seq % tq == 0
    nq = seq // tq
    n_chains = 2 * DIFF_HEADS
    vec = pl.BlockSpec((1, DIFF_QK_DIM), lambda b, qi: (0, 0))
    return pl.pallas_call(
        functools.partial(_attn_kernel, lam_init=lam_init),
        grid=(bsz, nq),
        in_specs=[
            pl.BlockSpec((tq, DIFF_WIDTH), lambda b, qi: (b * nq + qi, 0)),
            pl.BlockSpec((seq, DIFF_WIDTH), lambda b, qi: (b, 1)),
            pl.BlockSpec((None, nq, DIFF_WIDTH, tq), lambda b, qi: (b, 0, 0, 0)),
            pl.BlockSpec((DIFF_HEADS, 2 * tq, tq), lambda b, qi: (0, 0, 0)),
            vec, vec, vec, vec,
            pl.BlockSpec((DIFF_V_DIM, 1), lambda b, qi: (0, 0)),
        ],
        out_specs=pl.BlockSpec((tq, DIFF_WIDTH), lambda b, qi: (b * nq + qi, 0)),
        out_shape=jax.ShapeDtypeStruct((bsz * seq, DIFF_WIDTH), BF16),
        scratch_shapes=[
            pltpu.VMEM((n_chains, 1, tq), F32),
            pltpu.VMEM((n_chains, 1, tq), F32),
            pltpu.VMEM((n_chains, 1, tq), F32),
            pltpu.VMEM((n_chains, tq, tq), BF16),
            pltpu.VMEM((n_chains, DIFF_V_DIM, tq), F32),
        ],
        compiler_params=_compiler_params(("arbitrary", "arbitrary")),
        name="diff_attention",
    )(zqk, zqk, vt, bias_tile, lq1, lk1, lq2, lk2, subln_g_col)


def _mix_out_kernel(od_ref, gu_ref, gv_ref, x_ref, lng_ref, lnb_ref, ws_ref, bs_ref,
                    wo_ref, h1_ref, gm_sc):
    tm = od_ref.shape[0]
    gv = gv_ref[...].astype(F32)
    mu = jnp.mean(gv, axis=-1, keepdims=True)
    xc = gv - mu
    vn = (xc * lax.rsqrt(jnp.mean(xc * xc, axis=-1, keepdims=True) + EPS)
          * lng_ref[...] + lnb_ref[...]).astype(BF16)

    row = lax.broadcasted_iota(jnp.int32, (GMLP_CHUNK, GMLP_CHUNK), 0)
    col = lax.broadcasted_iota(jnp.int32, (GMLP_CHUNK, GMLP_CHUNK), 1)
    causal = (col // CHUNK) <= (row // CHUNK)
    for g in range(GMLP_GROUPS):
        ws_g = jnp.where(causal, ws_ref[g], 0.0).astype(BF16)
        cols = slice(g * GMLP_GROUP_DIM, (g + 1) * GMLP_GROUP_DIM)
        for r in range(tm // GMLP_CHUNK):
            rows = slice(r * GMLP_CHUNK, (r + 1) * GMLP_CHUNK)
            s = _dot(ws_g, vn[rows, cols]) + bs_ref[g]
            gm_sc[rows, cols] = (gu_ref[rows, cols].astype(F32) * s).astype(BF16)

    h1_ref[...] = (x_ref[...] + _dot(od_ref[...], wo_ref[:DIFF_WIDTH, :])
                   + _dot(gm_sc[...], wo_ref[DIFF_WIDTH:, :]))


def _mix_out(o_diff, zug, x2d, ln_g, ln_b, ws, bs, w_out_bf16):
    m, d = x2d.shape
    tm = MIX_TM
    assert m % tm == 0 and tm % GMLP_CHUNK == 0
    full = lambda shape: pl.BlockSpec(shape, lambda i: (0,) * len(shape))
    return pl.pallas_call(
        _mix_out_kernel,
        grid=(m // tm,),
        in_specs=[
            pl.BlockSpec((tm, DIFF_WIDTH), lambda i: (i, 0)),
            pl.BlockSpec((tm, GMLP_WIDTH), lambda i: (i, 0)),
            pl.BlockSpec((tm, GMLP_WIDTH), lambda i: (i, 1)),
            pl.BlockSpec((tm, d), lambda i: (i, 0)),
            full((1, GMLP_WIDTH)), full((1, GMLP_WIDTH)),
            full((GMLP_GROUPS, GMLP_CHUNK, GMLP_CHUNK)),
            full((GMLP_GROUPS, GMLP_CHUNK, 1)),
            full((DIFF_WIDTH + GMLP_WIDTH, d)),
        ],
        out_specs=pl.BlockSpec((tm, d), lambda i: (i, 0)),
        out_shape=jax.ShapeDtypeStruct((m, d), F32),
        scratch_shapes=[pltpu.VMEM((tm, GMLP_WIDTH), BF16)],
        compiler_params=_compiler_params(("arbitrary",)),
        name="mix_out",
    )(o_diff, zug, zug, x2d, ln_g, ln_b, ws, bs, w_out_bf16)


def _ffn_kernel(h1_ref, fg_ref, w1_ref, w3_ref, w2_ref, o_ref, f_sc):
    k = pl.program_id(1)

    def hidden_tile(f, base):
        act = (jax.nn.silu(_dot(f, w1_ref[...])) * _dot(f, w3_ref[...])).astype(BF16)
        o_ref[...] = base + _dot(act, w2_ref[...])

    @pl.when(k == 0)
    def _():
        h1 = h1_ref[...]
        f = (_rms_scale(h1) * fg_ref[...]).astype(BF16)
        f_sc[...] = f
        hidden_tile(f, h1)

    @pl.when(k > 0)
    def _():
        hidden_tile(f_sc[...], o_ref[...])


def _ffn(h1, ffn_g, w1, w3, w2):
    m, d = h1.shape
    hid = w1.shape[1]
    tm, th = FFN_TM, FFN_TH
    assert m % tm == 0 and hid % th == 0
    return pl.pallas_call(
        _ffn_kernel,
        grid=(m // tm, hid // th),
        in_specs=[
            pl.BlockSpec((tm, d), lambda i, k: (i, 0)),
            pl.BlockSpec((1, d), lambda i, k: (0, 0)),
            pl.BlockSpec((d, th), lambda i, k: (0, k)),
            pl.BlockSpec((d, th), lambda i, k: (0, k)),
            pl.BlockSpec((th, d), lambda i, k: (k, 0)),
        ],
        out_specs=pl.BlockSpec((tm, d), lambda i, k: (i, 0)),
        out_shape=jax.ShapeDtypeStruct((m, d), F32),
        scratch_shapes=[pltpu.VMEM((tm, d), BF16)],
        compiler_params=_compiler_params(("arbitrary", "arbitrary")),
        name="ffn",
    )(h1, ffn_g, w1, w3, w2)


def _ple_kernel(h2_ref, p_ref, wg_ref, wu_ref, g_ref, o_ref, *, final_norm):
    h2 = h2_ref[...]
    gate = _dot(h2.astype(BF16), wg_ref[...])
    up = _dot(p_ref[...].astype(BF16), wu_ref[...])
    h3 = h2 + up * jax.nn.sigmoid(gate)
    if final_norm:
        h3 = _rms_scale(h3) * g_ref[...]
    o_ref[...] = h3


def _ple(h2, p2d, w_gate, w_up, final_g, final_norm):
    m, d = h2.shape
    pd = p2d.shape[1]
    tm = PLE_TM
    assert m % tm == 0
    full = lambda shape: pl.BlockSpec(shape, lambda i: (0,) * len(shape))
    return pl.pallas_call(
        functools.partial(_ple_kernel, final_norm=final_norm),
        grid=(m // tm,),
        in_specs=[
            pl.BlockSpec((tm, d), lambda i: (i, 0)),
            pl.BlockSpec((tm, pd), lambda i: (i, 0)),
            full((d, d)), full((pd, d)), full((1, d)),
        ],
        out_specs=pl.BlockSpec((tm, d), lambda i: (i, 0)),
        out_shape=jax.ShapeDtypeStruct((m, d), F32),
        compiler_params=_compiler_params(("arbitrary",)),
        name="ple",
    )(h2, p2d, w_gate, w_up, final_g)


def kernel(x, p, w_in, w_out, attn_norm_g, ffn_norm_g, final_norm_g, lambda_q1, lambda_k1,
           lambda_q2, lambda_k2, subln_g, rel_bias, gmlp_ln_g, gmlp_ln_b, gmlp_ws, gmlp_b,
           ffn_w1, ffn_w3, ffn_w2, pl_w_up, pl_w_gate):
    bsz, seq, d = x.shape
    depth = w_in.shape[0]
    qk_cols = DIFF_HEADS * 2 * DIFF_QK_DIM
    assert w_in.shape[2] == 2 * qk_cols + DIFF_WIDTH + 2 * GMLP_WIDTH
    assert seq % GMLP_CHUNK == 0
    row = lambda v: v.reshape(1, -1)

    bias_tile = _bias_tile(rel_bias, seq)
    h = x.reshape(bsz * seq, d)
    for i in range(depth):
        lam_init = 0.8 - 0.6 * math.exp(-0.3 * i)
        w = w_in[i]
        zqk, vt, zug = _in_proj(
            h, row(attn_norm_g[i]), w[:, :2 * qk_cols].astype(BF16),
            w[:, 2 * qk_cols:2 * qk_cols + DIFF_WIDTH].T.astype(BF16),
            w[:, 2 * qk_cols + DIFF_WIDTH:].astype(BF16), bsz, seq)
        o_diff = _attention(zqk, vt, bias_tile, row(lambda_q1[i]), row(lambda_k1[i]),
                            row(lambda_q2[i]), row(lambda_k2[i]), subln_g[i].reshape(-1, 1),
                            bsz, seq, lam_init)
        h1 = _mix_out(o_diff, zug, h, row(gmlp_ln_g[i]), row(gmlp_ln_b[i]), gmlp_ws[i],
                      gmlp_b[i].reshape(GMLP_GROUPS, GMLP_CHUNK, 1), w_out[i].astype(BF16))
        h2 = _ffn(h1, row(ffn_norm_g[i]), ffn_w1[i].astype(BF16), ffn_w3[i].astype(BF16),
                  ffn_w2[i].astype(BF16))
        h = _ple(h2, p[i].reshape(bsz * seq, -1), pl_w_gate[i].astype(BF16),
                 pl_w_up[i].astype(BF16), row(final_norm_g), final_norm=(i == depth - 1))
    return h.reshape(bsz, seq, d)
```

```python
import functools
import math

import numpy as np
import jax
import jax.numpy as jnp
from jax import lax
from jax.experimental import pallas as pl
from jax.experimental.pallas import tpu as pltpu

F32 = jnp.float32
BF16 = jnp.bfloat16

CHUNK = 64
DIFF_HEADS = 4
DIFF_QK_DIM = 128
DIFF_V_DIM = 2 * DIFF_QK_DIM
DIFF_WIDTH = DIFF_HEADS * DIFF_V_DIM
GMLP_CHUNK = 128
GMLP_GROUPS = 8
GMLP_GROUP_DIM = 128
GMLP_WIDTH = GMLP_GROUPS * GMLP_GROUP_DIM
NUM_BUCKETS = 32
MAX_DISTANCE = 128
EPS = 1e-6
NEG_INF = -1e30
ATTN_SCALE = DIFF_QK_DIM ** -0.5
LOG2_E = math.log2(math.e)
Q_SCALE = ATTN_SCALE * LOG2_E

VMEM_LIMIT_BYTES = 56 * 1024 * 1024

IN_TM, IN_TN = 512, 1024
ATTN_TQ = 256
MIX_TM = 512
FFN_TM, FFN_TH = 1024, 512
PLE_TM = 512

_NT_DIMS = (((1,), (1,)), ((), ()))


def _dot(a, b):
    return jnp.dot(a, b, preferred_element_type=F32)


def _rms_scale(x):
    return x * lax.rsqrt(jnp.mean(x * x, axis=-1, keepdims=True) + EPS)


def _compiler_params(semantics):
    return pltpu.CompilerParams(dimension_semantics=semantics,
                                vmem_limit_bytes=VMEM_LIMIT_BYTES)


def _in_proj_kernel(x_ref, g_ref, wqk_ref, wvt_ref, wug_ref, qk_ref, vt_ref, ug_ref):
    a = (_rms_scale(x_ref[...]) * g_ref[...]).astype(BF16)
    for c in range(0, wqk_ref.shape[1], IN_TN):
        zc = _dot(a, wqk_ref[:, c:c + IN_TN])
        if c < DIFF_WIDTH:
            zc = zc * Q_SCALE
        qk_ref[:, c:c + IN_TN] = zc.astype(BF16)
    vt = lax.dot_general(wvt_ref[...], a, _NT_DIMS, preferred_element_type=F32).astype(BF16)
    for t in range(vt_ref.shape[0]):
        vt_ref[t] = vt[:, t * ATTN_TQ:(t + 1) * ATTN_TQ]
    for c in range(0, wug_ref.shape[1], IN_TN):
        ug_ref[:, c:c + IN_TN] = jax.nn.gelu(_dot(a, wug_ref[:, c:c + IN_TN])).astype(BF16)


def _in_proj(x2d, g, w_qk, w_vt, w_ug, bsz, seq):
    m, d = x2d.shape
    tm = IN_TM
    assert seq % tm == 0 and tm % ATTN_TQ == 0
    assert w_qk.shape[1] % IN_TN == 0 and w_ug.shape[1] % IN_TN == 0 and DIFF_WIDTH % IN_TN == 0
    tiles_per_seq = seq // tm
    kt = tm // ATTN_TQ
    resident = lambda shape: pl.BlockSpec(shape, lambda i: (0, 0), pipeline_mode=pl.Buffered(1))
    return pl.pallas_call(
        _in_proj_kernel,
        grid=(m // tm,),
        in_specs=[
            pl.BlockSpec((tm, d), lambda i: (i, 0)),
            pl.BlockSpec((1, d), lambda i: (0, 0)),
            resident(w_qk.shape), resident(w_vt.shape), resident(w_ug.shape),
        ],
        out_specs=[
            pl.BlockSpec((tm, w_qk.shape[1]), lambda i: (i, 0)),
            pl.BlockSpec((None, kt, DIFF_WIDTH, ATTN_TQ),
                         lambda i: (i // tiles_per_seq, i % tiles_per_seq, 0, 0)),
            pl.BlockSpec((tm, w_ug.shape[1]), lambda i: (i, 0)),
        ],
        out_shape=[
            jax.ShapeDtypeStruct((m, w_qk.shape[1]), BF16),
            jax.ShapeDtypeStruct((bsz, seq // ATTN_TQ, DIFF_WIDTH, ATTN_TQ), BF16),
            jax.ShapeDtypeStruct((m, w_ug.shape[1]), BF16),
        ],
        compiler_params=_compiler_params(("arbitrary",)),
        name="in_proj",
    )(x2d, g, w_qk, w_vt, w_ug)


def _t5_bucket_np(rel):
    half = NUM_BUCKETS // 2
    max_exact = half // 2
    ret = np.where(rel > 0, half, 0)
    n = np.abs(rel)
    nf = np.maximum(n, 1).astype(np.float32)
    large = max_exact + (np.log(nf / np.float32(max_exact))
                         / np.float32(math.log(MAX_DISTANCE / max_exact))
                         * np.float32(half - max_exact)).astype(np.int32)
    large = np.minimum(large, half - 1)
    return (ret + np.where(n < max_exact, n, large)).astype(np.int32)


def _near_bucket_table(seq):
    i = np.arange(ATTN_TQ)[None, :]
    j = np.arange(2 * ATTN_TQ)[:, None] - ATTN_TQ
    visible = np.floor_divide(j, CHUNK) <= i // CHUNK
    table = np.where(visible, _t5_bucket_np(j - i), -1).astype(np.int32)
    far = np.unique(_t5_bucket_np(-np.arange(ATTN_TQ + 1, max(seq, ATTN_TQ + 2))))
    assert far.size == 1, "far keys must share one bias bucket"
    return table, int(far[0])


def _bias_tile_kernel(bucket_ref, rb_ref, out_ref, *, buckets, far_bucket):
    bucket = bucket_ref[...]
    for h in range(DIFF_HEADS):
        shift = rb_ref[far_bucket, h]
        tile = jnp.full(bucket.shape, NEG_INF, F32)
        for b in buckets:
            tile = jnp.where(bucket == b, (rb_ref[b, h] - shift) * LOG2_E, tile)
        out_ref[h] = tile


def _bias_tile(rel_bias, seq):
    table, far_bucket = _near_bucket_table(seq)
    buckets = tuple(int(b) for b in np.unique(table) if b >= 0)
    return pl.pallas_call(
        functools.partial(_bias_tile_kernel, buckets=buckets, far_bucket=far_bucket),
        in_specs=[pl.BlockSpec(memory_space=pltpu.VMEM),
                  pl.BlockSpec(memory_space=pltpu.SMEM)],
        out_specs=pl.BlockSpec(memory_space=pltpu.VMEM),
        out_shape=jax.ShapeDtypeStruct((DIFF_HEADS,) + table.shape, F32),
        name="bias_tile",
    )(jnp.asarray(table), rel_bias)


def _attn_kernel(q_ref, k_ref, vt_ref, bias_ref, lq1_ref, lk1_ref, lq2_ref, lk2_ref,
                 g_ref, o_ref, m_sc, l_sc, acc_sc, *, n_q_tiles, lam_init):
    tq = ATTN_TQ
    qi = pl.program_id(1)
    chains = [(h, m) for h in range(DIFF_HEADS) for m in range(2)]

    def key_tile(kt, bias_rows, init):
        for c, (h, _) in enumerate(chains):
            cols = slice(c * DIFF_QK_DIM, (c + 1) * DIFF_QK_DIM)
            s = lax.dot_general(k_ref[kt * tq:(kt + 1) * tq, cols], q_ref[:, cols], _NT_DIMS,
                                preferred_element_type=F32)
            if bias_rows is not None:
                s = s + bias_ref[h, bias_rows, :]
            m_cur = jnp.max(s, axis=0, keepdims=True)
            if init:
                m_next = m_cur
            else:
                m_prev = m_sc[c]
                m_next = jnp.maximum(m_prev, m_cur)
            p = jnp.exp2(s - m_next)
            p_sum = jnp.sum(p, axis=0, keepdims=True)
            pv = _dot(vt_ref[kt, h * DIFF_V_DIM:(h + 1) * DIFF_V_DIM, :], p.astype(BF16))
            if init:
                l_sc[c] = p_sum
                acc_sc[c] = pv
            else:
                alpha = jnp.exp2(m_prev - m_next)
                l_sc[c] = alpha * l_sc[c] + p_sum
                acc_sc[c] = acc_sc[c] * alpha + pv
            m_sc[c] = m_next

    def query_tile(j):
        key_tile(j, slice(tq, 2 * tq), init=True)
        if j > 0:
            key_tile(j - 1, slice(0, tq), init=False)
        for kt in range(j - 2, -1, -1):
            key_tile(kt, None, init=False)

        lam = (jnp.exp(jnp.sum(lq1_ref[...] * lk1_ref[...], axis=-1, keepdims=True))
               - jnp.exp(jnp.sum(lq2_ref[...] * lk2_ref[...], axis=-1, keepdims=True))
               + lam_init)
        for h in range(DIFF_HEADS):
            o = acc_sc[2 * h] / l_sc[2 * h] - lam * (acc_sc[2 * h + 1] / l_sc[2 * h + 1])
            y = (o * lax.rsqrt(jnp.mean(o * o, axis=0, keepdims=True) + EPS)
                 * g_ref[...] * (1.0 - lam_init))
            o_ref[:, h * DIFF_V_DIM:(h + 1) * DIFF_V_DIM] = y.T.astype(BF16)

    for j in range(n_q_tiles):
        pl.when(qi == j)(functools.partial(query_tile, j))


def _attention(zqk, vt, bias_tile, lq1, lk1, lq2, lk2, subln_g_col, bsz, seq, lam_init):
    tq = ATTN_TQ
    assert seq % tq == 0
    nq = seq // tq
    n_chains = 2 * DIFF_HEADS
    vec = pl.BlockSpec((1, DIFF_QK_DIM), lambda b, qi: (0, 0))
    return pl.pallas_call(
        functools.partial(_attn_kernel, n_q_tiles=nq, lam_init=lam_init),
        grid=(bsz, nq),
        in_specs=[
            pl.BlockSpec((tq, DIFF_WIDTH), lambda b, qi: (b * nq + qi, 0)),
            pl.BlockSpec((seq, DIFF_WIDTH), lambda b, qi: (b, 1)),
            pl.BlockSpec((None, nq, DIFF_WIDTH, tq), lambda b, qi: (b, 0, 0, 0)),
            pl.BlockSpec((DIFF_HEADS, 2 * tq, tq), lambda b, qi: (0, 0, 0)),
            vec, vec, vec, vec,
            pl.BlockSpec((DIFF_V_DIM, 1), lambda b, qi: (0, 0)),
        ],
        out_specs=pl.BlockSpec((tq, DIFF_WIDTH), lambda b, qi: (b * nq + qi, 0)),
        out_shape=jax.ShapeDtypeStruct((bsz * seq, DIFF_WIDTH), BF16),
        scratch_shapes=[
            pltpu.VMEM((n_chains, 1, tq), F32),
            pltpu.VMEM((n_chains, 1, tq), F32),
            pltpu.VMEM((n_chains, DIFF_V_DIM, tq), F32),
        ],
        compiler_params=_compiler_params(("arbitrary", "arbitrary")),
        name="diff_attention",
    )(zqk, zqk, vt, bias_tile, lq1, lk1, lq2, lk2, subln_g_col)


def _mix_out_kernel(od_ref, gu_ref, gv_ref, x_ref, lng_ref, lnb_ref, ws_ref, bs_ref,
                    wo_ref, h1_ref, gm_sc):
    tm = od_ref.shape[0]
    gv = gv_ref[...].astype(F32)
    mu = jnp.mean(gv, axis=-1, keepdims=True)
    xc = gv - mu
    vn = (xc * lax.rsqrt(jnp.mean(xc * xc, axis=-1, keepdims=True) + EPS)
          * lng_ref[...] + lnb_ref[...]).astype(BF16)

    row = lax.broadcasted_iota(jnp.int32, (GMLP_CHUNK, GMLP_CHUNK), 0)
    col = lax.broadcasted_iota(jnp.int32, (GMLP_CHUNK, GMLP_CHUNK), 1)
    causal = (col // CHUNK) <= (row // CHUNK)
    for g in range(GMLP_GROUPS):
        ws_g = jnp.where(causal, ws_ref[g], 0.0).astype(BF16)
        cols = slice(g * GMLP_GROUP_DIM, (g + 1) * GMLP_GROUP_DIM)
        for r in range(tm // GMLP_CHUNK):
            rows = slice(r * GMLP_CHUNK, (r + 1) * GMLP_CHUNK)
            s = _dot(ws_g, vn[rows, cols]) + bs_ref[g]
            gm_sc[rows, cols] = (gu_ref[rows, cols].astype(F32) * s).astype(BF16)

    h1_ref[...] = (x_ref[...] + _dot(od_ref[...], wo_ref[:DIFF_WIDTH, :])
                   + _dot(gm_sc[...], wo_ref[DIFF_WIDTH:, :]))


def _mix_out(o_diff, zug, x2d, ln_g, ln_b, ws, bs, w_out_bf16):
    m, d = x2d.shape
    tm = MIX_TM
    assert m % tm == 0 and tm % GMLP_CHUNK == 0
    full = lambda shape: pl.BlockSpec(shape, lambda i: (0,) * len(shape))
    return pl.pallas_call(
        _mix_out_kernel,
        grid=(m // tm,),
        in_specs=[
            pl.BlockSpec((tm, DIFF_WIDTH), lambda i: (i, 0)),
            pl.BlockSpec((tm, GMLP_WIDTH), lambda i: (i, 0)),
            pl.BlockSpec((tm, GMLP_WIDTH), lambda i: (i, 1)),
            pl.BlockSpec((tm, d), lambda i: (i, 0)),
            full((1, GMLP_WIDTH)), full((1, GMLP_WIDTH)),
            full((GMLP_GROUPS, GMLP_CHUNK, GMLP_CHUNK)),
            full((GMLP_GROUPS, GMLP_CHUNK, 1)),
            full((DIFF_WIDTH + GMLP_WIDTH, d)),
        ],
        out_specs=pl.BlockSpec((tm, d), lambda i: (i, 0)),
        out_shape=jax.ShapeDtypeStruct((m, d), F32),
        scratch_shapes=[pltpu.VMEM((tm, GMLP_WIDTH), BF16)],
        compiler_params=_compiler_params(("arbitrary",)),
        name="mix_out",
    )(o_diff, zug, zug, x2d, ln_g, ln_b, ws, bs, w_out_bf16)


def _ffn_kernel(h1_ref, fg_ref, w1_ref, w3_ref, w2_ref, o_ref, f_sc):
    k = pl.program_id(1)

    def hidden_tile(f, base):
        act = (jax.nn.silu(_dot(f, w1_ref[...])) * _dot(f, w3_ref[...])).astype(BF16)
        o_ref[...] = base + _dot(act, w2_ref[...])

    @pl.when(k == 0)
    def _():
        h1 = h1_ref[...]
        f = (_rms_scale(h1) * fg_ref[...]).astype(BF16)
        f_sc[...] = f
        hidden_tile(f, h1)

    @pl.when(k > 0)
    def _():
        hidden_tile(f_sc[...], o_ref[...])


def _ffn(h1, ffn_g, w1, w3, w2):
    m, d = h1.shape
    hid = w1.shape[1]
    tm, th = FFN_TM, FFN_TH
    assert m % tm == 0 and hid % th == 0
    return pl.pallas_call(
        _ffn_kernel,
        grid=(m // tm, hid // th),
        in_specs=[
            pl.BlockSpec((tm, d), lambda i, k: (i, 0)),
            pl.BlockSpec((1, d), lambda i, k: (0, 0)),
            pl.BlockSpec((d, th), lambda i, k: (0, k)),
            pl.BlockSpec((d, th), lambda i, k: (0, k)),
            pl.BlockSpec((th, d), lambda i, k: (k, 0)),
        ],
        out_specs=pl.BlockSpec((tm, d), lambda i, k: (i, 0)),
        out_shape=jax.ShapeDtypeStruct((m, d), F32),
        scratch_shapes=[pltpu.VMEM((tm, d), BF16)],
        compiler_params=_compiler_params(("arbitrary", "arbitrary")),
        name="ffn",
    )(h1, ffn_g, w1, w3, w2)


def _ple_kernel(h2_ref, p_ref, wg_ref, wu_ref, g_ref, o_ref, *, final_norm):
    h2 = h2_ref[...]
    gate = _dot(h2.astype(BF16), wg_ref[...])
    up = _dot(p_ref[...].astype(BF16), wu_ref[...])
    h3 = h2 + up * jax.nn.sigmoid(gate)
    if final_norm:
        h3 = _rms_scale(h3) * g_ref[...]
    o_ref[...] = h3


def _ple(h2, p2d, w_gate, w_up, final_g, final_norm):
    m, d = h2.shape
    pd = p2d.shape[1]
    tm = PLE_TM
    assert m % tm == 0
    full = lambda shape: pl.BlockSpec(shape, lambda i: (0,) * len(shape))
    return pl.pallas_call(
        functools.partial(_ple_kernel, final_norm=final_norm),
        grid=(m // tm,),
        in_specs=[
            pl.BlockSpec((tm, d), lambda i: (i, 0)),
            pl.BlockSpec((tm, pd), lambda i: (i, 0)),
            full((d, d)), full((pd, d)), full((1, d)),
        ],
        out_specs=pl.BlockSpec((tm, d), lambda i: (i, 0)),
        out_shape=jax.ShapeDtypeStruct((m, d), F32),
        compiler_params=_compiler_params(("arbitrary",)),
        name="ple",
    )(h2, p2d, w_gate, w_up, final_g)


def kernel(x, p, w_in, w_out, attn_norm_g, ffn_norm_g, final_norm_g, lambda_q1, lambda_k1,
           lambda_q2, lambda_k2, subln_g, rel_bias, gmlp_ln_g, gmlp_ln_b, gmlp_ws, gmlp_b,
           ffn_w1, ffn_w3, ffn_w2, pl_w_up, pl_w_gate):
    bsz, seq, d = x.shape
    depth = w_in.shape[0]
    qk_cols = DIFF_HEADS * 2 * DIFF_QK_DIM
    assert w_in.shape[2] == 2 * qk_cols + DIFF_WIDTH + 2 * GMLP_WIDTH
    assert seq % GMLP_CHUNK == 0
    row = lambda v: v.reshape(1, -1)

    bias_tile = _bias_tile(rel_bias, seq)
    h = x.reshape(bsz * seq, d)
    for i in range(depth):
        lam_init = 0.8 - 0.6 * math.exp(-0.3 * i)
        w = w_in[i]
        zqk, vt, zug = _in_proj(
            h, row(attn_norm_g[i]), w[:, :2 * qk_cols].astype(BF16),
            w[:, 2 * qk_cols:2 * qk_cols + DIFF_WIDTH].T.astype(BF16),
            w[:, 2 * qk_cols + DIFF_WIDTH:].astype(BF16), bsz, seq)
        o_diff = _attention(zqk, vt, bias_tile, row(lambda_q1[i]), row(lambda_k1[i]),
                            row(lambda_q2[i]), row(lambda_k2[i]), subln_g[i].reshape(-1, 1),
                            bsz, seq, lam_init)
        h1 = _mix_out(o_diff, zug, h, row(gmlp_ln_g[i]), row(gmlp_ln_b[i]), gmlp_ws[i],
                      gmlp_b[i].reshape(GMLP_GROUPS, GMLP_CHUNK, 1), w_out[i].astype(BF16))
        h2 = _ffn(h1, row(ffn_norm_g[i]), ffn_w1[i].astype(BF16), ffn_w3[i].astype(BF16),
                  ffn_w2[i].astype(BF16))
        h = _ple(h2, p[i].reshape(bsz * seq, -1), pl_w_gate[i].astype(BF16),
                 pl_w_up[i].astype(BF16), row(final_norm_g), final_norm=(i == depth - 1))
    return h.reshape(bsz, seq, d)
```

```python
import functools
import math

import numpy as np
import jax
import jax.numpy as jnp
from jax import lax
from jax.experimental import pallas as pl
from jax.experimental.pallas import tpu as pltpu

F32 = jnp.float32
BF16 = jnp.bfloat16

CHUNK = 64
DIFF_HEADS = 4
DIFF_QK_DIM = 128
DIFF_V_DIM = 2 * DIFF_QK_DIM
DIFF_WIDTH = DIFF_HEADS * DIFF_V_DIM
GMLP_CHUNK = 128
GMLP_GROUPS = 8
GMLP_GROUP_DIM = 128
GMLP_WIDTH = GMLP_GROUPS * GMLP_GROUP_DIM
NUM_BUCKETS = 32
MAX_DISTANCE = 128
EPS = 1e-6
NEG_INF = -1e30
ATTN_SCALE = DIFF_QK_DIM ** -0.5
LOG2_E = math.log2(math.e)
Q_SCALE = ATTN_SCALE * LOG2_E

VMEM_LIMIT_BYTES = 56 * 1024 * 1024
BF16_SUBLANES = 16

IN_TM, IN_TN = 512, 1024
ATTN_TQ = 256
MIX_TM = 512
FFN_TM, FFN_TH = 1024, 512
PLE_TM = 512

_NT_DIMS = (((1,), (1,)), ((), ()))


def _dot(a, b):
    return jnp.dot(a, b, preferred_element_type=F32)


def _rms_scale(x):
    return x * lax.rsqrt(jnp.mean(x * x, axis=-1, keepdims=True) + EPS)


def _compiler_params(semantics):
    return pltpu.CompilerParams(dimension_semantics=semantics,
                                vmem_limit_bytes=VMEM_LIMIT_BYTES)


def _in_proj_kernel(x_ref, g_ref, wqk_ref, wvt_ref, wug_ref, *rest, n_cast):
    w_f32_refs = rest[:n_cast]
    qk_ref, vt_ref, ug_ref = rest[n_cast:n_cast + 3]
    w_bf16_refs = rest[n_cast + 3:]
    for src, dst in zip(w_f32_refs, w_bf16_refs):
        dst[...] = src[...].astype(BF16)
    a = (_rms_scale(x_ref[...]) * g_ref[...]).astype(BF16)
    for c in range(0, wqk_ref.shape[1], IN_TN):
        zc = _dot(a, wqk_ref[:, c:c + IN_TN])
        if c < DIFF_WIDTH:
            zc = zc * Q_SCALE
        qk_ref[:, c:c + IN_TN] = zc.astype(BF16)
    vt = lax.dot_general(wvt_ref[...], a, _NT_DIMS, preferred_element_type=F32).astype(BF16)
    for t in range(vt_ref.shape[0]):
        vt_ref[t] = vt[:, t * ATTN_TQ:(t + 1) * ATTN_TQ]
    for c in range(0, wug_ref.shape[1], IN_TN):
        ug_ref[:, c:c + IN_TN] = jax.nn.gelu(_dot(a, wug_ref[:, c:c + IN_TN])).astype(BF16)


def _slab_specs(weights, n_steps, index_map):
    specs = []
    for w in weights:
        rows, cols = w.shape
        assert rows % (n_steps * BF16_SUBLANES) == 0
        specs.append(pl.BlockSpec((rows // n_steps, cols), index_map))
    return specs


def _in_proj(x2d, g, w_qk, w_vt, w_ug, bsz, seq, cast_weights):
    m, d = x2d.shape
    tm = IN_TM
    assert seq % tm == 0 and tm % ATTN_TQ == 0
    assert w_qk.shape[1] % IN_TN == 0 and w_ug.shape[1] % IN_TN == 0 and DIFF_WIDTH % IN_TN == 0
    tiles_per_seq = seq // tm
    kt = tm // ATTN_TQ
    resident = lambda shape: pl.BlockSpec(shape, lambda i: (0, 0), pipeline_mode=pl.Buffered(1))
    slab_specs = _slab_specs(cast_weights, m // tm, lambda i: (i, 0))
    outs = pl.pallas_call(
        functools.partial(_in_proj_kernel, n_cast=len(cast_weights)),
        grid=(m // tm,),
        in_specs=[
            pl.BlockSpec((tm, d), lambda i: (i, 0)),
            pl.BlockSpec((1, d), lambda i: (0, 0)),
            resident(w_qk.shape), resident(w_vt.shape), resident(w_ug.shape),
        ] + slab_specs,
        out_specs=[
            pl.BlockSpec((tm, w_qk.shape[1]), lambda i: (i, 0)),
            pl.BlockSpec((None, kt, DIFF_WIDTH, ATTN_TQ),
                         lambda i: (i // tiles_per_seq, i % tiles_per_seq, 0, 0)),
            pl.BlockSpec((tm, w_ug.shape[1]), lambda i: (i, 0)),
        ] + slab_specs,
        out_shape=[
            jax.ShapeDtypeStruct((m, w_qk.shape[1]), BF16),
            jax.ShapeDtypeStruct((bsz, seq // ATTN_TQ, DIFF_WIDTH, ATTN_TQ), BF16),
            jax.ShapeDtypeStruct((m, w_ug.shape[1]), BF16),
        ] + [jax.ShapeDtypeStruct(w.shape, BF16) for w in cast_weights],
        compiler_params=_compiler_params(("arbitrary",)),
        name="in_proj",
    )(x2d, g, w_qk, w_vt, w_ug, *cast_weights)
    return outs[0], outs[1], outs[2], outs[3:]


def _t5_bucket_np(rel):
    half = NUM_BUCKETS // 2
    max_exact = half // 2
    ret = np.where(rel > 0, half, 0)
    n = np.abs(rel)
    nf = np.maximum(n, 1).astype(np.float32)
    large = max_exact + (np.log(nf / np.float32(max_exact))
                         / np.float32(math.log(MAX_DISTANCE / max_exact))
                         * np.float32(half - max_exact)).astype(np.int32)
    large = np.minimum(large, half - 1)
    return (ret + np.where(n < max_exact, n, large)).astype(np.int32)


def _near_bucket_table(seq):
    i = np.arange(ATTN_TQ)[None, :]
    j = np.arange(2 * ATTN_TQ)[:, None] - ATTN_TQ
    visible = np.floor_divide(j, CHUNK) <= i // CHUNK
    table = np.where(visible, _t5_bucket_np(j - i), -1).astype(np.int32)
    far = np.unique(_t5_bucket_np(-np.arange(ATTN_TQ + 1, max(seq, ATTN_TQ + 2))))
    assert far.size == 1, "far keys must share one bias bucket"
    return table, int(far[0])


def _bias_tile_kernel(bucket_ref, rb_ref, out_ref, *, buckets, far_bucket):
    bucket = bucket_ref[...]
    for h in range(DIFF_HEADS):
        shift = rb_ref[far_bucket, h]
        tile = jnp.full(bucket.shape, NEG_INF, F32)
        for b in buckets:
            tile = jnp.where(bucket == b, (rb_ref[b, h] - shift) * LOG2_E, tile)
        out_ref[h] = tile


def _bias_tile(rel_bias, seq):
    table, far_bucket = _near_bucket_table(seq)
    buckets = tuple(int(b) for b in np.unique(table) if b >= 0)
    return pl.pallas_call(
        functools.partial(_bias_tile_kernel, buckets=buckets, far_bucket=far_bucket),
        in_specs=[pl.BlockSpec(memory_space=pltpu.VMEM),
                  pl.BlockSpec(memory_space=pltpu.SMEM)],
        out_specs=pl.BlockSpec(memory_space=pltpu.VMEM),
        out_shape=jax.ShapeDtypeStruct((DIFF_HEADS,) + table.shape, F32),
        name="bias_tile",
    )(jnp.asarray(table), rel_bias)


def _attn_kernel(q_ref, k_ref, vt_ref, bias_ref, lq1_ref, lk1_ref, lq2_ref, lk2_ref,
                 g_ref, *rest, n_q_tiles, n_cast, lam_init):
    w_f32_refs = rest[:n_cast]
    o_ref = rest[n_cast]
    w_bf16_refs = rest[n_cast + 1:2 * n_cast + 1]
    m_sc, l_sc, acc_sc = rest[2 * n_cast + 1:]
    tq = ATTN_TQ
    qi = pl.program_id(1)
    chains = [(h, m) for h in range(DIFF_HEADS) for m in range(2)]

    def key_tile(kt, bias_rows, init):
        for c, (h, _) in enumerate(chains):
            cols = slice(c * DIFF_QK_DIM, (c + 1) * DIFF_QK_DIM)
            s = lax.dot_general(k_ref[kt * tq:(kt + 1) * tq, cols], q_ref[:, cols], _NT_DIMS,
                                preferred_element_type=F32)
            if bias_rows is not None:
                s = s + bias_ref[h, bias_rows, :]
            m_cur = jnp.max(s, axis=0, keepdims=True)
            if init:
                m_next = m_cur
            else:
                m_prev = m_sc[c]
                m_next = jnp.maximum(m_prev, m_cur)
            p = jnp.exp2(s - m_next)
            p_sum = jnp.sum(p, axis=0, keepdims=True)
            pv = _dot(vt_ref[kt, h * DIFF_V_DIM:(h + 1) * DIFF_V_DIM, :], p.astype(BF16))
            if init:
                l_sc[c] = p_sum
                acc_sc[c] = pv
            else:
                alpha = jnp.exp2(m_prev - m_next)
                l_sc[c] = alpha * l_sc[c] + p_sum
                acc_sc[c] = acc_sc[c] * alpha + pv
            m_sc[c] = m_next

    def query_tile(j):
        for src, dst in zip(w_f32_refs, w_bf16_refs):
            dst[...] = src[...].astype(BF16)
        key_tile(j, slice(tq, 2 * tq), init=True)
        if j > 0:
            key_tile(j - 1, slice(0, tq), init=False)
        for kt in range(j - 2, -1, -1):
            key_tile(kt, None, init=False)

        lam = (jnp.exp(jnp.sum(lq1_ref[...] * lk1_ref[...], axis=-1, keepdims=True))
               - jnp.exp(jnp.sum(lq2_ref[...] * lk2_ref[...], axis=-1, keepdims=True))
               + lam_init)
        for h in range(DIFF_HEADS):
            o = acc_sc[2 * h] / l_sc[2 * h] - lam * (acc_sc[2 * h + 1] / l_sc[2 * h + 1])
            y = (o * lax.rsqrt(jnp.mean(o * o, axis=0, keepdims=True) + EPS)
                 * g_ref[...] * (1.0 - lam_init))
            o_ref[:, h * DIFF_V_DIM:(h + 1) * DIFF_V_DIM] = y.T.astype(BF16)

    for j in range(n_q_tiles):
        pl.when(qi == j)(functools.partial(query_tile, j))


def _attention(zqk, vt, bias_tile, lq1, lk1, lq2, lk2, subln_g_col, bsz, seq, lam_init,
               cast_weights):
    tq = ATTN_TQ
    assert seq % tq == 0
    nq = seq // tq
    n_chains = 2 * DIFF_HEADS
    vec = pl.BlockSpec((1, DIFF_QK_DIM), lambda b, qi: (0, 0))
    slab_specs = _slab_specs(cast_weights, bsz * nq, lambda b, qi: (b * nq + qi, 0))
    outs = pl.pallas_call(
        functools.partial(_attn_kernel, n_q_tiles=nq, n_cast=len(cast_weights),
                          lam_init=lam_init),
        grid=(bsz, nq),
        in_specs=[
            pl.BlockSpec((tq, DIFF_WIDTH), lambda b, qi: (b * nq + qi, 0)),
            pl.BlockSpec((seq, DIFF_WIDTH), lambda b, qi: (b, 1)),
            pl.BlockSpec((None, nq, DIFF_WIDTH, tq), lambda b, qi: (b, 0, 0, 0)),
            pl.BlockSpec((DIFF_HEADS, 2 * tq, tq), lambda b, qi: (0, 0, 0)),
            vec, vec, vec, vec,
            pl.BlockSpec((DIFF_V_DIM, 1), lambda b, qi: (0, 0)),
        ] + slab_specs,
        out_specs=[pl.BlockSpec((tq, DIFF_WIDTH), lambda b, qi: (b * nq + qi, 0))] + slab_specs,
        out_shape=[jax.ShapeDtypeStruct((bsz * seq, DIFF_WIDTH), BF16)]
        + [jax.ShapeDtypeStruct(w.shape, BF16) for w in cast_weights],
        scratch_shapes=[
            pltpu.VMEM((n_chains, 1, tq), F32),
            pltpu.VMEM((n_chains, 1, tq), F32),
            pltpu.VMEM((n_chains, DIFF_V_DIM, tq), F32),
        ],
        compiler_params=_compiler_params(("arbitrary", "arbitrary")),
        name="diff_attention",
    )(zqk, zqk, vt, bias_tile, lq1, lk1, lq2, lk2, subln_g_col, *cast_weights)
    return outs[0], outs[1:]


def _mix_out_kernel(od_ref, gu_ref, gv_ref, x_ref, lng_ref, lnb_ref, ws_ref, bs_ref,
                    wo_ref, h1_ref, gm_sc):
    tm = od_ref.shape[0]
    gv = gv_ref[...].astype(F32)
    mu = jnp.mean(gv, axis=-1, keepdims=True)
    xc = gv - mu
    vn = (xc * lax.rsqrt(jnp.mean(xc * xc, axis=-1, keepdims=True) + EPS)
          * lng_ref[...] + lnb_ref[...]).astype(BF16)

    row = lax.broadcasted_iota(jnp.int32, (GMLP_CHUNK, GMLP_CHUNK), 0)
    col = lax.broadcasted_iota(jnp.int32, (GMLP_CHUNK, GMLP_CHUNK), 1)
    causal = (col // CHUNK) <= (row // CHUNK)
    for g in range(GMLP_GROUPS):
        ws_g = jnp.where(causal, ws_ref[g], 0.0).astype(BF16)
        cols = slice(g * GMLP_GROUP_DIM, (g + 1) * GMLP_GROUP_DIM)
        for r in range(tm // GMLP_CHUNK):
            rows = slice(r * GMLP_CHUNK, (r + 1) * GMLP_CHUNK)
            s = _dot(ws_g, vn[rows, cols]) + bs_ref[g]
            gm_sc[rows, cols] = (gu_ref[rows, cols].astype(F32) * s).astype(BF16)

    h1_ref[...] = (x_ref[...] + _dot(od_ref[...], wo_ref[:DIFF_WIDTH, :])
                   + _dot(gm_sc[...], wo_ref[DIFF_WIDTH:, :]))


def _mix_out(o_diff, zug, x2d, ln_g, ln_b, ws, bs, w_out_bf16):
    m, d = x2d.shape
    tm = MIX_TM
    assert m % tm == 0 and tm % GMLP_CHUNK == 0
    full = lambda shape: pl.BlockSpec(shape, lambda i: (0,) * len(shape))
    return pl.pallas_call(
        _mix_out_kernel,
        grid=(m // tm,),
        in_specs=[
            pl.BlockSpec((tm, DIFF_WIDTH), lambda i: (i, 0)),
            pl.BlockSpec((tm, GMLP_WIDTH), lambda i: (i, 0)),
            pl.BlockSpec((tm, GMLP_WIDTH), lambda i: (i, 1)),
            pl.BlockSpec((tm, d), lambda i: (i, 0)),
            full((1, GMLP_WIDTH)), full((1, GMLP_WIDTH)),
            full((GMLP_GROUPS, GMLP_CHUNK, GMLP_CHUNK)),
            full((GMLP_GROUPS, GMLP_CHUNK, 1)),
            full((DIFF_WIDTH + GMLP_WIDTH, d)),
        ],
        out_specs=pl.BlockSpec((tm, d), lambda i: (i, 0)),
        out_shape=jax.ShapeDtypeStruct((m, d), F32),
        scratch_shapes=[pltpu.VMEM((tm, GMLP_WIDTH), BF16)],
        compiler_params=_compiler_params(("arbitrary",)),
        name="mix_out",
    )(o_diff, zug, zug, x2d, ln_g, ln_b, ws, bs, w_out_bf16)


def _ffn_kernel(h1_ref, fg_ref, w1_ref, w3_ref, w2_ref, o_ref, f_sc):
    k = pl.program_id(1)

    def hidden_tile(f, base):
        act = (jax.nn.silu(_dot(f, w1_ref[...])) * _dot(f, w3_ref[...])).astype(BF16)
        o_ref[...] = base + _dot(act, w2_ref[...])

    @pl.when(k == 0)
    def _():
        h1 = h1_ref[...]
        f = (_rms_scale(h1) * fg_ref[...]).astype(BF16)
        f_sc[...] = f
        hidden_tile(f, h1)

    @pl.when(k > 0)
    def _():
        hidden_tile(f_sc[...], o_ref[...])


def _ffn(h1, ffn_g, w1, w3, w2):
    m, d = h1.shape
    hid = w1.shape[1]
    tm, th = FFN_TM, FFN_TH
    assert m % tm == 0 and hid % th == 0
    return pl.pallas_call(
        _ffn_kernel,
        grid=(m // tm, hid // th),
        in_specs=[
            pl.BlockSpec((tm, d), lambda i, k: (i, 0)),
            pl.BlockSpec((1, d), lambda i, k: (0, 0)),
            pl.BlockSpec((d, th), lambda i, k: (0, k)),
            pl.BlockSpec((d, th), lambda i, k: (0, k)),
            pl.BlockSpec((th, d), lambda i, k: (k, 0)),
        ],
        out_specs=pl.BlockSpec((tm, d), lambda i, k: (i, 0)),
        out_shape=jax.ShapeDtypeStruct((m, d), F32),
        scratch_shapes=[pltpu.VMEM((tm, d), BF16)],
        compiler_params=_compiler_params(("arbitrary", "arbitrary")),
        name="ffn",
    )(h1, ffn_g, w1, w3, w2)


def _ple_kernel(h2_ref, p_ref, wg_ref, wu_ref, g_ref, o_ref, *, final_norm):
    h2 = h2_ref[...]
    gate = _dot(h2.astype(BF16), wg_ref[...])
    up = _dot(p_ref[...].astype(BF16), wu_ref[...])
    h3 = h2 + up * jax.nn.sigmoid(gate)
    if final_norm:
        h3 = _rms_scale(h3) * g_ref[...]
    o_ref[...] = h3


def _ple(h2, p2d, w_gate, w_up, final_g, final_norm):
    m, d = h2.shape
    pd = p2d.shape[1]
    tm = PLE_TM
    assert m % tm == 0
    full = lambda shape: pl.BlockSpec(shape, lambda i: (0,) * len(shape))
    return pl.pallas_call(
        functools.partial(_ple_kernel, final_norm=final_norm),
        grid=(m // tm,),
        in_specs=[
            pl.BlockSpec((tm, d), lambda i: (i, 0)),
            pl.BlockSpec((tm, pd), lambda i: (i, 0)),
            full((d, d)), full((pd, d)), full((1, d)),
        ],
        out_specs=pl.BlockSpec((tm, d), lambda i: (i, 0)),
        out_shape=jax.ShapeDtypeStruct((m, d), F32),
        compiler_params=_compiler_params(("arbitrary",)),
        name="ple",
    )(h2, p2d, w_gate, w_up, final_g)


def kernel(x, p, w_in, w_out, attn_norm_g, ffn_norm_g, final_norm_g, lambda_q1, lambda_k1,
           lambda_q2, lambda_k2, subln_g, rel_bias, gmlp_ln_g, gmlp_ln_b, gmlp_ws, gmlp_b,
           ffn_w1, ffn_w3, ffn_w2, pl_w_up, pl_w_gate):
    bsz, seq, d = x.shape
    depth = w_in.shape[0]
    qk_cols = DIFF_HEADS * 2 * DIFF_QK_DIM
    assert w_in.shape[2] == 2 * qk_cols + DIFF_WIDTH + 2 * GMLP_WIDTH
    assert seq % GMLP_CHUNK == 0
    row = lambda v: v.reshape(1, -1)

    bias_tile = _bias_tile(rel_bias, seq)
    h = x.reshape(bsz * seq, d)
    for i in range(depth):
        lam_init = 0.8 - 0.6 * math.exp(-0.3 * i)
        w = w_in[i]
        zqk, vt, zug, (w2,) = _in_proj(
            h, row(attn_norm_g[i]), w[:, :2 * qk_cols].astype(BF16),
            w[:, 2 * qk_cols:2 * qk_cols + DIFF_WIDTH].T.astype(BF16),
            w[:, 2 * qk_cols + DIFF_WIDTH:].astype(BF16), bsz, seq, cast_weights=(ffn_w2[i],))
        o_diff, (w1, w3, wo, wg) = _attention(
            zqk, vt, bias_tile, row(lambda_q1[i]), row(lambda_k1[i]), row(lambda_q2[i]),
            row(lambda_k2[i]), subln_g[i].reshape(-1, 1), bsz, seq, lam_init,
            cast_weights=(ffn_w1[i], ffn_w3[i], w_out[i], pl_w_gate[i]))
        h1 = _mix_out(o_diff, zug, h, row(gmlp_ln_g[i]), row(gmlp_ln_b[i]), gmlp_ws[i],
                      gmlp_b[i].reshape(GMLP_GROUPS, GMLP_CHUNK, 1), wo)
        h2 = _ffn(h1, row(ffn_norm_g[i]), w1, w3, w2)
        h = _ple(h2, p[i].reshape(bsz * seq, -1), wg, pl_w_up[i].astype(BF16),
                 row(final_norm_g), final_norm=(i == depth - 1))
    return h.reshape(bsz, seq, d)
```

```python
import functools
import math

import numpy as np
import jax
import jax.numpy as jnp
from jax import lax
from jax.experimental import pallas as pl
from jax.experimental.pallas import tpu as pltpu

F32 = jnp.float32
BF16 = jnp.bfloat16

CHUNK = 64
DIFF_HEADS = 4
DIFF_QK_DIM = 128
DIFF_V_DIM = 2 * DIFF_QK_DIM
DIFF_WIDTH = DIFF_HEADS * DIFF_V_DIM
GMLP_CHUNK = 128
GMLP_GROUPS = 8
GMLP_GROUP_DIM = 128
GMLP_WIDTH = GMLP_GROUPS * GMLP_GROUP_DIM
NUM_BUCKETS = 32
MAX_DISTANCE = 128
EPS = 1e-6
NEG_INF = -1e30
ATTN_SCALE = DIFF_QK_DIM ** -0.5
LOG2_E = math.log2(math.e)
Q_SCALE = ATTN_SCALE * LOG2_E

VMEM_LIMIT_BYTES = 56 * 1024 * 1024
BF16_SUBLANES = 16

IN_TM, IN_TN = 512, 1024
ATTN_TQ = 256
MIX_TM = 512
FFN_TM, FFN_TH = 1024, 512
PLE_TM = 512

_NT_DIMS = (((1,), (1,)), ((), ()))


def _dot(a, b):
    return jnp.dot(a, b, preferred_element_type=F32)


def _rms_scale(x):
    return x * lax.rsqrt(jnp.mean(x * x, axis=-1, keepdims=True) + EPS)


def _compiler_params(semantics):
    return pltpu.CompilerParams(dimension_semantics=semantics,
                                vmem_limit_bytes=VMEM_LIMIT_BYTES)


def _in_proj_kernel(x_ref, g_ref, wqk_ref, wvt_ref, wug_ref, *rest, n_cast):
    w_f32_refs = rest[:n_cast]
    qk_ref, vt_ref, ug_ref = rest[n_cast:n_cast + 3]
    w_bf16_refs = rest[n_cast + 3:]
    for src, dst in zip(w_f32_refs, w_bf16_refs):
        dst[...] = src[...].astype(BF16)
    a = (_rms_scale(x_ref[...]) * g_ref[...]).astype(BF16)
    for c in range(0, wqk_ref.shape[1], IN_TN):
        zc = _dot(a, wqk_ref[:, c:c + IN_TN])
        if c < DIFF_WIDTH:
            zc = zc * Q_SCALE
        qk_ref[:, c:c + IN_TN] = zc.astype(BF16)
    vt = lax.dot_general(wvt_ref[...], a, _NT_DIMS, preferred_element_type=F32).astype(BF16)
    for t in range(vt_ref.shape[0]):
        vt_ref[t] = vt[:, t * ATTN_TQ:(t + 1) * ATTN_TQ]
    for c in range(0, wug_ref.shape[1], IN_TN):
        ug_ref[:, c:c + IN_TN] = jax.nn.gelu(_dot(a, wug_ref[:, c:c + IN_TN])).astype(BF16)


def _slab_specs(weights, n_steps, index_map):
    specs = []
    for w in weights:
        rows, cols = w.shape
        assert rows % (n_steps * BF16_SUBLANES) == 0
        specs.append(pl.BlockSpec((rows // n_steps, cols), index_map))
    return specs


def _in_proj(x2d, g, w_qk, w_vt, w_ug, bsz, seq, cast_weights):
    m, d = x2d.shape
    tm = IN_TM
    assert seq % tm == 0 and tm % ATTN_TQ == 0
    assert w_qk.shape[1] % IN_TN == 0 and w_ug.shape[1] % IN_TN == 0 and DIFF_WIDTH % IN_TN == 0
    tiles_per_seq = seq // tm
    kt = tm // ATTN_TQ
    resident = lambda shape: pl.BlockSpec(shape, lambda i: (0, 0), pipeline_mode=pl.Buffered(1))
    slab_specs = _slab_specs(cast_weights, m // tm, lambda i: (i, 0))
    outs = pl.pallas_call(
        functools.partial(_in_proj_kernel, n_cast=len(cast_weights)),
        grid=(m // tm,),
        in_specs=[
            pl.BlockSpec((tm, d), lambda i: (i, 0)),
            pl.BlockSpec((1, d), lambda i: (0, 0)),
            resident(w_qk.shape), resident(w_vt.shape), resident(w_ug.shape),
        ] + slab_specs,
        out_specs=[
            pl.BlockSpec((tm, w_qk.shape[1]), lambda i: (i, 0)),
            pl.BlockSpec((None, kt, DIFF_WIDTH, ATTN_TQ),
                         lambda i: (i // tiles_per_seq, i % tiles_per_seq, 0, 0)),
            pl.BlockSpec((tm, w_ug.shape[1]), lambda i: (i, 0)),
        ] + slab_specs,
        out_shape=[
            jax.ShapeDtypeStruct((m, w_qk.shape[1]), BF16),
            jax.ShapeDtypeStruct((bsz, seq // ATTN_TQ, DIFF_WIDTH, ATTN_TQ), BF16),
            jax.ShapeDtypeStruct((m, w_ug.shape[1]), BF16),
        ] + [jax.ShapeDtypeStruct(w.shape, BF16) for w in cast_weights],
        compiler_params=_compiler_params(("arbitrary",)),
        name="in_proj",
    )(x2d, g, w_qk, w_vt, w_ug, *cast_weights)
    return outs[0], outs[1], outs[2], outs[3:]


def _t5_bucket_np(rel):
    half = NUM_BUCKETS // 2
    max_exact = half // 2
    ret = np.where(rel > 0, half, 0)
    n = np.abs(rel)
    nf = np.maximum(n, 1).astype(np.float32)
    large = max_exact + (np.log(nf / np.float32(max_exact))
                         / np.float32(math.log(MAX_DISTANCE / max_exact))
                         * np.float32(half - max_exact)).astype(np.int32)
    large = np.minimum(large, half - 1)
    return (ret + np.where(n < max_exact, n, large)).astype(np.int32)


def _near_bucket_table(seq):
    i = np.arange(ATTN_TQ)[None, :]
    j = np.arange(2 * ATTN_TQ)[:, None] - ATTN_TQ
    visible = np.floor_divide(j, CHUNK) <= i // CHUNK
    table = np.where(visible, _t5_bucket_np(j - i), -1).astype(np.int32)
    far = np.unique(_t5_bucket_np(-np.arange(ATTN_TQ + 1, max(seq, ATTN_TQ + 2))))
    assert far.size == 1, "far keys must share one bias bucket"
    return table, int(far[0])


def _bias_tile_kernel(bucket_ref, rb_ref, out_ref, *, buckets, far_bucket):
    bucket = bucket_ref[...]
    for h in range(DIFF_HEADS):
        shift = rb_ref[far_bucket, h]
        tile = jnp.full(bucket.shape, NEG_INF, F32)
        for b in buckets:
            tile = jnp.where(bucket == b, (rb_ref[b, h] - shift) * LOG2_E, tile)
        out_ref[h] = tile


def _bias_tile(rel_bias, seq):
    table, far_bucket = _near_bucket_table(seq)
    buckets = tuple(int(b) for b in np.unique(table) if b >= 0)
    return pl.pallas_call(
        functools.partial(_bias_tile_kernel, buckets=buckets, far_bucket=far_bucket),
        in_specs=[pl.BlockSpec(memory_space=pltpu.VMEM),
                  pl.BlockSpec(memory_space=pltpu.SMEM)],
        out_specs=pl.BlockSpec(memory_space=pltpu.VMEM),
        out_shape=jax.ShapeDtypeStruct((DIFF_HEADS,) + table.shape, F32),
        name="bias_tile",
    )(jnp.asarray(table), rel_bias)


def _attn_kernel(q_ref, k_ref, vt_ref, bias_ref, lq1_ref, lk1_ref, lq2_ref, lk2_ref,
                 g_ref, *rest, n_q_tiles, n_cast, lam_init):
    w_f32_refs = rest[:n_cast]
    o_ref = rest[n_cast]
    w_bf16_refs = rest[n_cast + 1:2 * n_cast + 1]
    m_sc, l_sc, acc_sc = rest[2 * n_cast + 1:]
    tq = ATTN_TQ
    qi = pl.program_id(1)
    chains = [(h, m) for h in range(DIFF_HEADS) for m in range(2)]

    def key_tile(kt, bias_rows, init):
        for c, (h, _) in enumerate(chains):
            cols = slice(c * DIFF_QK_DIM, (c + 1) * DIFF_QK_DIM)
            s = lax.dot_general(k_ref[kt * tq:(kt + 1) * tq, cols], q_ref[:, cols], _NT_DIMS,
                                preferred_element_type=F32)
            if bias_rows is not None:
                s = s + bias_ref[h, bias_rows, :]
            m_cur = jnp.max(s, axis=0, keepdims=True)
            if init:
                m_next = m_cur
            else:
                m_prev = m_sc[c]
                m_next = jnp.maximum(m_prev, m_cur)
            p = jnp.exp2(s - m_next)
            p_sum = jnp.sum(p, axis=0, keepdims=True)
            pv = _dot(vt_ref[kt, h * DIFF_V_DIM:(h + 1) * DIFF_V_DIM, :], p.astype(BF16))
            if init:
                l_sc[c] = p_sum
                acc_sc[c] = pv
            else:
                alpha = jnp.exp2(m_prev - m_next)
                l_sc[c] = alpha * l_sc[c] + p_sum
                acc_sc[c] = acc_sc[c] * alpha + pv
            m_sc[c] = m_next

    def query_tile(j):
        for src, dst in zip(w_f32_refs, w_bf16_refs):
            dst[...] = src[...].astype(BF16)
        key_tile(j, slice(tq, 2 * tq), init=True)
        if j > 0:
            key_tile(j - 1, slice(0, tq), init=False)
        for kt in range(j - 2, -1, -1):
            key_tile(kt, None, init=False)

        lam = (jnp.exp(jnp.sum(lq1_ref[...] * lk1_ref[...], axis=-1, keepdims=True))
               - jnp.exp(jnp.sum(lq2_ref[...] * lk2_ref[...], axis=-1, keepdims=True))
               + lam_init)
        for h in range(DIFF_HEADS):
            o = (acc_sc[2 * h] * (1.0 / l_sc[2 * h])
                 - acc_sc[2 * h + 1] * (lam / l_sc[2 * h + 1]))
            r = lax.rsqrt(jnp.mean(o * o, axis=0, keepdims=True) + EPS) * (1.0 - lam_init)
            y = o * r * g_ref[...]
            o_ref[:, h * DIFF_V_DIM:(h + 1) * DIFF_V_DIM] = y.T.astype(BF16)

    for j in range(n_q_tiles):
        pl.when(qi == j)(functools.partial(query_tile, j))


def _attention(zqk, vt, bias_tile, lq1, lk1, lq2, lk2, subln_g_col, bsz, seq, lam_init,
               cast_weights):
    tq = ATTN_TQ
    assert seq % tq == 0
    nq = seq // tq
    n_chains = 2 * DIFF_HEADS
    vec = pl.BlockSpec((1, DIFF_QK_DIM), lambda b, qi: (0, 0))
    slab_specs = _slab_specs(cast_weights, bsz * nq, lambda b, qi: (b * nq + qi, 0))
    outs = pl.pallas_call(
        functools.partial(_attn_kernel, n_q_tiles=nq, n_cast=len(cast_weights),
                          lam_init=lam_init),
        grid=(bsz, nq),
        in_specs=[
            pl.BlockSpec((tq, DIFF_WIDTH), lambda b, qi: (b * nq + qi, 0)),
            pl.BlockSpec((seq, DIFF_WIDTH), lambda b, qi: (b, 1)),
            pl.BlockSpec((None, nq, DIFF_WIDTH, tq), lambda b, qi: (b, 0, 0, 0)),
            pl.BlockSpec((DIFF_HEADS, 2 * tq, tq), lambda b, qi: (0, 0, 0)),
            vec, vec, vec, vec,
            pl.BlockSpec((DIFF_V_DIM, 1), lambda b, qi: (0, 0)),
        ] + slab_specs,
        out_specs=[pl.BlockSpec((tq, DIFF_WIDTH), lambda b, qi: (b * nq + qi, 0))] + slab_specs,
        out_shape=[jax.ShapeDtypeStruct((bsz * seq, DIFF_WIDTH), BF16)]
        + [jax.ShapeDtypeStruct(w.shape, BF16) for w in cast_weights],
        scratch_shapes=[
            pltpu.VMEM((n_chains, 1, tq), F32),
            pltpu.VMEM((n_chains, 1, tq), F32),
            pltpu.VMEM((n_chains, DIFF_V_DIM, tq), F32),
        ],
        compiler_params=_compiler_params(("arbitrary", "arbitrary")),
        name="diff_attention",
    )(zqk, zqk, vt, bias_tile, lq1, lk1, lq2, lk2, subln_g_col, *cast_weights)
    return outs[0], outs[1:]


def _mix_out_kernel(od_ref, gu_ref, gv_ref, x_ref, lng_ref, lnb_ref, ws_ref, bs_ref,
                    wo_ref, h1_ref, gm_sc):
    tm = od_ref.shape[0]
    h1_attn = x_ref[...] + _dot(od_ref[...], wo_ref[:DIFF_WIDTH, :])
    gv = gv_ref[...].astype(F32)
    mu = jnp.mean(gv, axis=-1, keepdims=True)
    xc = gv - mu
    vn = (xc * lax.rsqrt(jnp.mean(xc * xc, axis=-1, keepdims=True) + EPS)
          * lng_ref[...] + lnb_ref[...]).astype(BF16)

    row = lax.broadcasted_iota(jnp.int32, (GMLP_CHUNK, GMLP_CHUNK), 0)
    col = lax.broadcasted_iota(jnp.int32, (GMLP_CHUNK, GMLP_CHUNK), 1)
    causal = (col // CHUNK) <= (row // CHUNK)
    for g in range(GMLP_GROUPS):
        ws_g = jnp.where(causal, ws_ref[g], 0.0).astype(BF16)
        cols = slice(g * GMLP_GROUP_DIM, (g + 1) * GMLP_GROUP_DIM)
        for r in range(tm // GMLP_CHUNK):
            rows = slice(r * GMLP_CHUNK, (r + 1) * GMLP_CHUNK)
            s = _dot(ws_g, vn[rows, cols]) + bs_ref[g]
            gm_sc[rows, cols] = (gu_ref[rows, cols].astype(F32) * s).astype(BF16)

    h1_ref[...] = h1_attn + _dot(gm_sc[...], wo_ref[DIFF_WIDTH:, :])


def _mix_out(o_diff, zug, x2d, ln_g, ln_b, ws, bs, w_out_bf16):
    m, d = x2d.shape
    tm = MIX_TM
    assert m % tm == 0 and tm % GMLP_CHUNK == 0
    full = lambda shape: pl.BlockSpec(shape, lambda i: (0,) * len(shape))
    return pl.pallas_call(
        _mix_out_kernel,
        grid=(m // tm,),
        in_specs=[
            pl.BlockSpec((tm, DIFF_WIDTH), lambda i: (i, 0)),
            pl.BlockSpec((tm, GMLP_WIDTH), lambda i: (i, 0)),
            pl.BlockSpec((tm, GMLP_WIDTH), lambda i: (i, 1)),
            pl.BlockSpec((tm, d), lambda i: (i, 0)),
            full((1, GMLP_WIDTH)), full((1, GMLP_WIDTH)),
            full((GMLP_GROUPS, GMLP_CHUNK, GMLP_CHUNK)),
            full((GMLP_GROUPS, GMLP_CHUNK, 1)),
            full((DIFF_WIDTH + GMLP_WIDTH, d)),
        ],
        out_specs=pl.BlockSpec((tm, d), lambda i: (i, 0)),
        out_shape=jax.ShapeDtypeStruct((m, d), F32),
        scratch_shapes=[pltpu.VMEM((tm, GMLP_WIDTH), BF16)],
        compiler_params=_compiler_params(("arbitrary",)),
        name="mix_out",
    )(o_diff, zug, zug, x2d, ln_g, ln_b, ws, bs, w_out_bf16)


def _ffn_kernel(h1_ref, fg_ref, w1_ref, w3_ref, w2_ref, o_ref, f_sc):
    k = pl.program_id(1)

    def hidden_tile(f, base):
        act = (jax.nn.silu(_dot(f, w1_ref[...])) * _dot(f, w3_ref[...])).astype(BF16)
        o_ref[...] = base + _dot(act, w2_ref[...])

    @pl.when(k == 0)
    def _():
        h1 = h1_ref[...]
        f = (_rms_scale(h1) * fg_ref[...]).astype(BF16)
        f_sc[...] = f
        hidden_tile(f, h1)

    @pl.when(k > 0)
    def _():
        hidden_tile(f_sc[...], o_ref[...])


def _ffn(h1, ffn_g, w1, w3, w2):
    m, d = h1.shape
    hid = w1.shape[1]
    tm, th = FFN_TM, FFN_TH
    assert m % tm == 0 and hid % th == 0
    return pl.pallas_call(
        _ffn_kernel,
        grid=(m // tm, hid // th),
        in_specs=[
            pl.BlockSpec((tm, d), lambda i, k: (i, 0)),
            pl.BlockSpec((1, d), lambda i, k: (0, 0)),
            pl.BlockSpec((d, th), lambda i, k: (0, k)),
            pl.BlockSpec((d, th), lambda i, k: (0, k)),
            pl.BlockSpec((th, d), lambda i, k: (k, 0)),
        ],
        out_specs=pl.BlockSpec((tm, d), lambda i, k: (i, 0)),
        out_shape=jax.ShapeDtypeStruct((m, d), F32),
        scratch_shapes=[pltpu.VMEM((tm, d), BF16)],
        compiler_params=_compiler_params(("arbitrary", "arbitrary")),
        name="ffn",
    )(h1, ffn_g, w1, w3, w2)


def _ple_kernel(h2_ref, p_ref, wg_ref, wu_ref, g_ref, o_ref, *, final_norm):
    h2 = h2_ref[...]
    gate = _dot(h2.astype(BF16), wg_ref[...])
    up = _dot(p_ref[...].astype(BF16), wu_ref[...])
    h3 = h2 + up * jax.nn.sigmoid(gate)
    if final_norm:
        h3 = _rms_scale(h3) * g_ref[...]
    o_ref[...] = h3


def _ple(h2, p2d, w_gate, w_up, final_g, final_norm):
    m, d = h2.shape
    pd = p2d.shape[1]
    tm = PLE_TM
    assert m % tm == 0
    full = lambda shape: pl.BlockSpec(shape, lambda i: (0,) * len(shape))
    return pl.pallas_call(
        functools.partial(_ple_kernel, final_norm=final_norm),
        grid=(m // tm,),
        in_specs=[
            pl.BlockSpec((tm, d), lambda i: (i, 0)),
            pl.BlockSpec((tm, pd), lambda i: (i, 0)),
            full((d, d)), full((pd, d)), full((1, d)),
        ],
        out_specs=pl.BlockSpec((tm, d), lambda i: (i, 0)),
        out_shape=jax.ShapeDtypeStruct((m, d), F32),
        compiler_params=_compiler_params(("arbitrary",)),
        name="ple",
    )(h2, p2d, w_gate, w_up, final_g)


def kernel(x, p, w_in, w_out, attn_norm_g, ffn_norm_g, final_norm_g, lambda_q1, lambda_k1,
           lambda_q2, lambda_k2, subln_g, rel_bias, gmlp_ln_g, gmlp_ln_b, gmlp_ws, gmlp_b,
           ffn_w1, ffn_w3, ffn_w2, pl_w_up, pl_w_gate):
    bsz, seq, d = x.shape
    depth = w_in.shape[0]
    qk_cols = DIFF_HEADS * 2 * DIFF_QK_DIM
    assert w_in.shape[2] == 2 * qk_cols + DIFF_WIDTH + 2 * GMLP_WIDTH
    assert seq % GMLP_CHUNK == 0
    row = lambda v: v.reshape(1, -1)

    bias_tile = _bias_tile(rel_bias, seq)
    h = x.reshape(bsz * seq, d)
    for i in range(depth):
        lam_init = 0.8 - 0.6 * math.exp(-0.3 * i)
        w = w_in[i]
        zqk, vt, zug, (w2,) = _in_proj(
            h, row(attn_norm_g[i]), w[:, :2 * qk_cols].astype(BF16),
            w[:, 2 * qk_cols:2 * qk_cols + DIFF_WIDTH].T.astype(BF16),
            w[:, 2 * qk_cols + DIFF_WIDTH:].astype(BF16), bsz, seq, cast_weights=(ffn_w2[i],))
        o_diff, (w1, w3, wo, wg) = _attention(
            zqk, vt, bias_tile, row(lambda_q1[i]), row(lambda_k1[i]), row(lambda_q2[i]),
            row(lambda_k2[i]), subln_g[i].reshape(-1, 1), bsz, seq, lam_init,
            cast_weights=(ffn_w1[i], ffn_w3[i], w_out[i], pl_w_gate[i]))
        h1 = _mix_out(o_diff, zug, h, row(gmlp_ln_g[i]), row(gmlp_ln_b[i]), gmlp_ws[i],
                      gmlp_b[i].reshape(GMLP_GROUPS, GMLP_CHUNK, 1), wo)
        h2 = _ffn(h1, row(ffn_norm_g[i]), w1, w3, w2)
        h = _ple(h2, p[i].reshape(bsz * seq, -1), wg, pl_w_up[i].astype(BF16),
                 row(final_norm_g), final_norm=(i == depth - 1))
    return h.reshape(bsz, seq, d)
```

```python
import functools
import math

import numpy as np
import jax
import jax.numpy as jnp
from jax import lax
from jax.experimental import pallas as pl
from jax.experimental.pallas import tpu as pltpu

F32 = jnp.float32
BF16 = jnp.bfloat16

CHUNK = 64
DIFF_HEADS = 4
DIFF_QK_DIM = 128
DIFF_V_DIM = 2 * DIFF_QK_DIM
DIFF_WIDTH = DIFF_HEADS * DIFF_V_DIM
GMLP_CHUNK = 128
GMLP_GROUPS = 8
GMLP_GROUP_DIM = 128
GMLP_WIDTH = GMLP_GROUPS * GMLP_GROUP_DIM
NUM_BUCKETS = 32
MAX_DISTANCE = 128
EPS = 1e-6
NEG_INF = -1e30
ATTN_SCALE = DIFF_QK_DIM ** -0.5
LOG2_E = math.log2(math.e)
Q_SCALE = ATTN_SCALE * LOG2_E

VMEM_LIMIT_BYTES = 56 * 1024 * 1024
BF16_SUBLANES = 16

IN_TM, IN_TN = 512, 1024
W_CHUNK = 128
ATTN_TQ = 256
MIX_TM = 512
FFN_TM, FFN_TH = 1024, 512
PLE_TM = 512

_NT_DIMS = (((1,), (1,)), ((), ()))


def _dot(a, b):
    return jnp.dot(a, b, preferred_element_type=F32)


def _rms_scale(x):
    return x * lax.rsqrt(jnp.mean(x * x, axis=-1, keepdims=True) + EPS)


def _compiler_params(semantics):
    return pltpu.CompilerParams(dimension_semantics=semantics,
                                vmem_limit_bytes=VMEM_LIMIT_BYTES)


def _in_proj_kernel(x_ref, g_ref, w_hbm, *rest, n_cast):
    w_f32_refs = rest[:n_cast]
    qk_ref, vt_ref, ug_ref = rest[n_cast:n_cast + 3]
    w_bf16_refs = rest[n_cast + 3:2 * n_cast + 3]
    wqk_sc, wvt_sc, wug_sc, stage_sc, sem = rest[2 * n_cast + 3:]
    n_qk, n_v = wqk_sc.shape[1], wvt_sc.shape[0]

    def chunk_copy(c):
        slot = c % 2
        return pltpu.make_async_copy(w_hbm.at[c * W_CHUNK:(c + 1) * W_CHUNK, :],
                                     stage_sc.at[slot], sem.at[slot])

    @pl.when(pl.program_id(0) == 0)
    def _():
        n_chunks = w_hbm.shape[0] // W_CHUNK
        chunk_copy(0).start()
        for c in range(n_chunks):
            if c + 1 < n_chunks:
                chunk_copy(c + 1).start()
            chunk_copy(c).wait()
            rows = slice(c * W_CHUNK, (c + 1) * W_CHUNK)
            blk = stage_sc[c % 2]
            wqk_sc[rows, :] = blk[:, :n_qk].astype(BF16)
            wvt_sc[:, rows] = blk[:, n_qk:n_qk + n_v].T.astype(BF16)
            wug_sc[rows, :] = blk[:, n_qk + n_v:].astype(BF16)

    for src, dst in zip(w_f32_refs, w_bf16_refs):
        dst[...] = src[...].astype(BF16)
    a = (_rms_scale(x_ref[...]) * g_ref[...]).astype(BF16)
    for c in range(0, n_qk, IN_TN):
        zc = _dot(a, wqk_sc[:, c:c + IN_TN])
        if c < DIFF_WIDTH:
            zc = zc * Q_SCALE
        qk_ref[:, c:c + IN_TN] = zc.astype(BF16)
    vt = lax.dot_general(wvt_sc[...], a, _NT_DIMS, preferred_element_type=F32).astype(BF16)
    for t in range(vt_ref.shape[0]):
        vt_ref[t] = vt[:, t * ATTN_TQ:(t + 1) * ATTN_TQ]
    for c in range(0, wug_sc.shape[1], IN_TN):
        ug_ref[:, c:c + IN_TN] = jax.nn.gelu(_dot(a, wug_sc[:, c:c + IN_TN])).astype(BF16)


def _slab_specs(weights, n_steps, index_map):
    specs = []
    for w in weights:
        rows, cols = w.shape
        assert rows % (n_steps * BF16_SUBLANES) == 0
        specs.append(pl.BlockSpec((rows // n_steps, cols), index_map))
    return specs


def _in_proj(x2d, g, w_in, bsz, seq, cast_weights):
    m, d = x2d.shape
    tm = IN_TM
    n_qk, n_v, n_ug = 2 * DIFF_WIDTH, DIFF_WIDTH, 2 * GMLP_WIDTH
    assert w_in.shape == (d, n_qk + n_v + n_ug) and d % W_CHUNK == 0
    assert seq % tm == 0 and tm % ATTN_TQ == 0
    assert n_qk % IN_TN == 0 and n_ug % IN_TN == 0 and DIFF_WIDTH % IN_TN == 0
    tiles_per_seq = seq // tm
    kt = tm // ATTN_TQ
    slab_specs = _slab_specs(cast_weights, m // tm, lambda i: (i, 0))
    outs = pl.pallas_call(
        functools.partial(_in_proj_kernel, n_cast=len(cast_weights)),
        grid=(m // tm,),
        in_specs=[
            pl.BlockSpec((tm, d), lambda i: (i, 0)),
            pl.BlockSpec((1, d), lambda i: (0, 0)),
            pl.BlockSpec(memory_space=pl.ANY),
        ] + slab_specs,
        out_specs=[
            pl.BlockSpec((tm, n_qk), lambda i: (i, 0)),
            pl.BlockSpec((None, kt, DIFF_WIDTH, ATTN_TQ),
                         lambda i: (i // tiles_per_seq, i % tiles_per_seq, 0, 0)),
            pl.BlockSpec((tm, n_ug), lambda i: (i, 0)),
        ] + slab_specs,
        out_shape=[
            jax.ShapeDtypeStruct((m, n_qk), BF16),
            jax.ShapeDtypeStruct((bsz, seq // ATTN_TQ, DIFF_WIDTH, ATTN_TQ), BF16),
            jax.ShapeDtypeStruct((m, n_ug), BF16),
        ] + [jax.ShapeDtypeStruct(w.shape, BF16) for w in cast_weights],
        scratch_shapes=[
            pltpu.VMEM((d, n_qk), BF16),
            pltpu.VMEM((n_v, d), BF16),
            pltpu.VMEM((d, n_ug), BF16),
            pltpu.VMEM((2, W_CHUNK, w_in.shape[1]), F32),
            pltpu.SemaphoreType.DMA((2,)),
        ],
        compiler_params=_compiler_params(("arbitrary",)),
        name="in_proj",
    )(x2d, g, w_in, *cast_weights)
    return outs[0], outs[1], outs[2], outs[3:]


def _t5_bucket_np(rel):
    half = NUM_BUCKETS // 2
    max_exact = half // 2
    ret = np.where(rel > 0, half, 0)
    n = np.abs(rel)
    nf = np.maximum(n, 1).astype(np.float32)
    large = max_exact + (np.log(nf / np.float32(max_exact))
                         / np.float32(math.log(MAX_DISTANCE / max_exact))
                         * np.float32(half - max_exact)).astype(np.int32)
    large = np.minimum(large, half - 1)
    return (ret + np.where(n < max_exact, n, large)).astype(np.int32)


def _near_bucket_table(seq):
    i = np.arange(ATTN_TQ)[None, :]
    j = np.arange(2 * ATTN_TQ)[:, None] - ATTN_TQ
    visible = np.floor_divide(j, CHUNK) <= i // CHUNK
    table = np.where(visible, _t5_bucket_np(j - i), -1).astype(np.int32)
    far = np.unique(_t5_bucket_np(-np.arange(ATTN_TQ + 1, max(seq, ATTN_TQ + 2))))
    assert far.size == 1, "far keys must share one bias bucket"
    return table, int(far[0])


def _bias_tile_kernel(bucket_ref, rb_ref, out_ref, *, buckets, far_bucket):
    bucket = bucket_ref[...]
    for h in range(DIFF_HEADS):
        shift = rb_ref[far_bucket, h]
        tile = jnp.full(bucket.shape, NEG_INF, F32)
        for b in buckets:
            tile = jnp.where(bucket == b, (rb_ref[b, h] - shift) * LOG2_E, tile)
        out_ref[h] = tile


def _bias_tile(rel_bias, seq):
    table, far_bucket = _near_bucket_table(seq)
    buckets = tuple(int(b) for b in np.unique(table) if b >= 0)
    return pl.pallas_call(
        functools.partial(_bias_tile_kernel, buckets=buckets, far_bucket=far_bucket),
        in_specs=[pl.BlockSpec(memory_space=pltpu.VMEM),
                  pl.BlockSpec(memory_space=pltpu.SMEM)],
        out_specs=pl.BlockSpec(memory_space=pltpu.VMEM),
        out_shape=jax.ShapeDtypeStruct((DIFF_HEADS,) + table.shape, F32),
        name="bias_tile",
    )(jnp.asarray(table), rel_bias)


def _attn_kernel(q_ref, k_ref, vt_ref, bias_ref, lq1_ref, lk1_ref, lq2_ref, lk2_ref,
                 g_ref, *rest, n_q_tiles, n_cast, lam_init):
    w_f32_refs = rest[:n_cast]
    o_ref = rest[n_cast]
    w_bf16_refs = rest[n_cast + 1:2 * n_cast + 1]
    m_sc, l_sc, acc_sc = rest[2 * n_cast + 1:]
    tq = ATTN_TQ
    qi = pl.program_id(1)
    chains = [(h, m) for h in range(DIFF_HEADS) for m in range(2)]

    def key_tile(kt, bias_rows, init):
        for c, (h, _) in enumerate(chains):
            cols = slice(c * DIFF_QK_DIM, (c + 1) * DIFF_QK_DIM)
            s = lax.dot_general(k_ref[kt * tq:(kt + 1) * tq, cols], q_ref[:, cols], _NT_DIMS,
                                preferred_element_type=F32)
            if bias_rows is not None:
                s = s + bias_ref[h, bias_rows, :]
            m_cur = jnp.max(s, axis=0, keepdims=True)
            if init:
                m_next = m_cur
            else:
                m_prev = m_sc[c]
                m_next = jnp.maximum(m_prev, m_cur)
            p = jnp.exp2(s - m_next)
            p_sum = jnp.sum(p, axis=0, keepdims=True)
            pv = _dot(vt_ref[kt, h * DIFF_V_DIM:(h + 1) * DIFF_V_DIM, :], p.astype(BF16))
            if init:
                l_sc[c] = p_sum
                acc_sc[c] = pv
            else:
                alpha = jnp.exp2(m_prev - m_next)
                l_sc[c] = alpha * l_sc[c] + p_sum
                acc_sc[c] = acc_sc[c] * alpha + pv
            m_sc[c] = m_next

    def query_tile(j):
        for src, dst in zip(w_f32_refs, w_bf16_refs):
            dst[...] = src[...].astype(BF16)
        key_tile(j, slice(tq, 2 * tq), init=True)
        if j > 0:
            key_tile(j - 1, slice(0, tq), init=False)
        for kt in range(j - 2, -1, -1):
            key_tile(kt, None, init=False)

        lam = (jnp.exp(jnp.sum(lq1_ref[...] * lk1_ref[...], axis=-1, keepdims=True))
               - jnp.exp(jnp.sum(lq2_ref[...] * lk2_ref[...], axis=-1, keepdims=True))
               + lam_init)
        for h in range(DIFF_HEADS):
            o = (acc_sc[2 * h] * (1.0 / l_sc[2 * h])
                 - acc_sc[2 * h + 1] * (lam / l_sc[2 * h + 1]))
            r = lax.rsqrt(jnp.mean(o * o, axis=0, keepdims=True) + EPS) * (1.0 - lam_init)
            y = o * r * g_ref[...]
            o_ref[:, h * DIFF_V_DIM:(h + 1) * DIFF_V_DIM] = y.T.astype(BF16)

    for j in range(n_q_tiles):
        pl.when(qi == j)(functools.partial(query_tile, j))


def _attention(zqk, vt, bias_tile, lq1, lk1, lq2, lk2, subln_g_col, bsz, seq, lam_init,
               cast_weights):
    tq = ATTN_TQ
    assert seq % tq == 0
    nq = seq // tq
    n_chains = 2 * DIFF_HEADS
    vec = pl.BlockSpec((1, DIFF_QK_DIM), lambda b, qi: (0, 0))
    slab_specs = _slab_specs(cast_weights, bsz * nq, lambda b, qi: (b * nq + qi, 0))
    outs = pl.pallas_call(
        functools.partial(_attn_kernel, n_q_tiles=nq, n_cast=len(cast_weights),
                          lam_init=lam_init),
        grid=(bsz, nq),
        in_specs=[
            pl.BlockSpec((tq, DIFF_WIDTH), lambda b, qi: (b * nq + qi, 0)),
            pl.BlockSpec((seq, DIFF_WIDTH), lambda b, qi: (b, 1)),
            pl.BlockSpec((None, nq, DIFF_WIDTH, tq), lambda b, qi: (b, 0, 0, 0)),
            pl.BlockSpec((DIFF_HEADS, 2 * tq, tq), lambda b, qi: (0, 0, 0)),
            vec, vec, vec, vec,
            pl.BlockSpec((DIFF_V_DIM, 1), lambda b, qi: (0, 0)),
        ] + slab_specs,
        out_specs=[pl.BlockSpec((tq, DIFF_WIDTH), lambda b, qi: (b * nq + qi, 0))] + slab_specs,
        out_shape=[jax.ShapeDtypeStruct((bsz * seq, DIFF_WIDTH), BF16)]
        + [jax.ShapeDtypeStruct(w.shape, BF16) for w in cast_weights],
        scratch_shapes=[
            pltpu.VMEM((n_chains, 1, tq), F32),
            pltpu.VMEM((n_chains, 1, tq), F32),
            pltpu.VMEM((n_chains, DIFF_V_DIM, tq), F32),
        ],
        compiler_params=_compiler_params(("arbitrary", "arbitrary")),
        name="diff_attention",
    )(zqk, zqk, vt, bias_tile, lq1, lk1, lq2, lk2, subln_g_col, *cast_weights)
    return outs[0], outs[1:]


def _mix_out_kernel(od_ref, gu_ref, gv_ref, x_ref, lng_ref, lnb_ref, ws_ref, bs_ref,
                    wo_ref, h1_ref, gm_sc):
    tm = od_ref.shape[0]
    h1_attn = x_ref[...] + _dot(od_ref[...], wo_ref[:DIFF_WIDTH, :])
    gv = gv_ref[...].astype(F32)
    mu = jnp.mean(gv, axis=-1, keepdims=True)
    xc = gv - mu
    vn = (xc * lax.rsqrt(jnp.mean(xc * xc, axis=-1, keepdims=True) + EPS)
          * lng_ref[...] + lnb_ref[...]).astype(BF16)

    row = lax.broadcasted_iota(jnp.int32, (GMLP_CHUNK, GMLP_CHUNK), 0)
    col = lax.broadcasted_iota(jnp.int32, (GMLP_CHUNK, GMLP_CHUNK), 1)
    causal = (col // CHUNK) <= (row // CHUNK)
    for g in range(GMLP_GROUPS):
        ws_g = jnp.where(causal, ws_ref[g], 0.0).astype(BF16)
        cols = slice(g * GMLP_GROUP_DIM, (g + 1) * GMLP_GROUP_DIM)
        for r in range(tm // GMLP_CHUNK):
            rows = slice(r * GMLP_CHUNK, (r + 1) * GMLP_CHUNK)
            s = _dot(ws_g, vn[rows, cols]) + bs_ref[g]
            gm_sc[rows, cols] = (gu_ref[rows, cols].astype(F32) * s).astype(BF16)

    h1_ref[...] = h1_attn + _dot(gm_sc[...], wo_ref[DIFF_WIDTH:, :])


def _mix_out(o_diff, zug, x2d, ln_g, ln_b, ws, bs, w_out_bf16):
    m, d = x2d.shape
    tm = MIX_TM
    assert m % tm == 0 and tm % GMLP_CHUNK == 0
    full = lambda shape: pl.BlockSpec(shape, lambda i: (0,) * len(shape))
    return pl.pallas_call(
        _mix_out_kernel,
        grid=(m // tm,),
        in_specs=[
            pl.BlockSpec((tm, DIFF_WIDTH), lambda i: (i, 0)),
            pl.BlockSpec((tm, GMLP_WIDTH), lambda i: (i, 0)),
            pl.BlockSpec((tm, GMLP_WIDTH), lambda i: (i, 1)),
            pl.BlockSpec((tm, d), lambda i: (i, 0)),
            full((1, GMLP_WIDTH)), full((1, GMLP_WIDTH)),
            full((GMLP_GROUPS, GMLP_CHUNK, GMLP_CHUNK)),
            full((GMLP_GROUPS, GMLP_CHUNK, 1)),
            full((DIFF_WIDTH + GMLP_WIDTH, d)),
        ],
        out_specs=pl.BlockSpec((tm, d), lambda i: (i, 0)),
        out_shape=jax.ShapeDtypeStruct((m, d), F32),
        scratch_shapes=[pltpu.VMEM((tm, GMLP_WIDTH), BF16)],
        compiler_params=_compiler_params(("arbitrary",)),
        name="mix_out",
    )(o_diff, zug, zug, x2d, ln_g, ln_b, ws, bs, w_out_bf16)


def _ffn_kernel(h1_ref, fg_ref, w1_ref, w3_ref, w2_ref, o_ref, f_sc):
    k = pl.program_id(1)

    def hidden_tile(f, base):
        act = (jax.nn.silu(_dot(f, w1_ref[...])) * _dot(f, w3_ref[...])).astype(BF16)
        o_ref[...] = base + _dot(act, w2_ref[...])

    @pl.when(k == 0)
    def _():
        h1 = h1_ref[...]
        f = (_rms_scale(h1) * fg_ref[...]).astype(BF16)
        f_sc[...] = f
        hidden_tile(f, h1)

    @pl.when(k > 0)
    def _():
        hidden_tile(f_sc[...], o_ref[...])


def _ffn(h1, ffn_g, w1, w3, w2):
    m, d = h1.shape
    hid = w1.shape[1]
    tm, th = FFN_TM, FFN_TH
    assert m % tm == 0 and hid % th == 0
    return pl.pallas_call(
        _ffn_kernel,
        grid=(m // tm, hid // th),
        in_specs=[
            pl.BlockSpec((tm, d), lambda i, k: (i, 0)),
            pl.BlockSpec((1, d), lambda i, k: (0, 0)),
            pl.BlockSpec((d, th), lambda i, k: (0, k)),
            pl.BlockSpec((d, th), lambda i, k: (0, k)),
            pl.BlockSpec((th, d), lambda i, k: (k, 0)),
        ],
        out_specs=pl.BlockSpec((tm, d), lambda i, k: (i, 0)),
        out_shape=jax.ShapeDtypeStruct((m, d), F32),
        scratch_shapes=[pltpu.VMEM((tm, d), BF16)],
        compiler_params=_compiler_params(("arbitrary", "arbitrary")),
        name="ffn",
    )(h1, ffn_g, w1, w3, w2)


def _ple_kernel(h2_ref, p_ref, wg_ref, wu_ref, g_ref, o_ref, *, final_norm):
    h2 = h2_ref[...]
    gate = _dot(h2.astype(BF16), wg_ref[...])
    up = _dot(p_ref[...].astype(BF16), wu_ref[...])
    h3 = h2 + up * jax.nn.sigmoid(gate)
    if final_norm:
        h3 = _rms_scale(h3) * g_ref[...]
    o_ref[...] = h3


def _ple(h2, p2d, w_gate, w_up, final_g, final_norm):
    m, d = h2.shape
    pd = p2d.shape[1]
    tm = PLE_TM
    assert m % tm == 0
    full = lambda shape: pl.BlockSpec(shape, lambda i: (0,) * len(shape))
    return pl.pallas_call(
        functools.partial(_ple_kernel, final_norm=final_norm),
        grid=(m // tm,),
        in_specs=[
            pl.BlockSpec((tm, d), lambda i: (i, 0)),
            pl.BlockSpec((tm, pd), lambda i: (i, 0)),
            full((d, d)), full((pd, d)), full((1, d)),
        ],
        out_specs=pl.BlockSpec((tm, d), lambda i: (i, 0)),
        out_shape=jax.ShapeDtypeStruct((m, d), F32),
        compiler_params=_compiler_params(("arbitrary",)),
        name="ple",
    )(h2, p2d, w_gate, w_up, final_g)


def kernel(x, p, w_in, w_out, attn_norm_g, ffn_norm_g, final_norm_g, lambda_q1, lambda_k1,
           lambda_q2, lambda_k2, subln_g, rel_bias, gmlp_ln_g, gmlp_ln_b, gmlp_ws, gmlp_b,
           ffn_w1, ffn_w3, ffn_w2, pl_w_up, pl_w_gate):
    bsz, seq, d = x.shape
    depth = w_in.shape[0]
    qk_cols = DIFF_HEADS * 2 * DIFF_QK_DIM
    assert w_in.shape[2] == 2 * qk_cols + DIFF_WIDTH + 2 * GMLP_WIDTH
    assert seq % GMLP_CHUNK == 0
    row = lambda v: v.reshape(1, -1)

    bias_tile = _bias_tile(rel_bias, seq)
    h = x.reshape(bsz * seq, d)
    for i in range(depth):
        lam_init = 0.8 - 0.6 * math.exp(-0.3 * i)
        zqk, vt, zug, (w2,) = _in_proj(h, row(attn_norm_g[i]), w_in[i], bsz, seq,
                                       cast_weights=(ffn_w2[i],))
        o_diff, (w1, w3, wo, wg) = _attention(
            zqk, vt, bias_tile, row(lambda_q1[i]), row(lambda_k1[i]), row(lambda_q2[i]),
            row(lambda_k2[i]), subln_g[i].reshape(-1, 1), bsz, seq, lam_init,
            cast_weights=(ffn_w1[i], ffn_w3[i], w_out[i], pl_w_gate[i]))
        h1 = _mix_out(o_diff, zug, h, row(gmlp_ln_g[i]), row(gmlp_ln_b[i]), gmlp_ws[i],
                      gmlp_b[i].reshape(GMLP_GROUPS, GMLP_CHUNK, 1), wo)
        h2 = _ffn(h1, row(ffn_norm_g[i]), w1, w3, w2)
        h = _ple(h2, p[i].reshape(bsz * seq, -1), wg, pl_w_up[i].astype(BF16),
                 row(final_norm_g), final_norm=(i == depth - 1))
    return h.reshape(bsz, seq, d)
```

```python
import functools
import math

import numpy as np
import jax
import jax.numpy as jnp
from jax import lax
from jax.experimental import pallas as pl
from jax.experimental.pallas import tpu as pltpu

F32 = jnp.float32
BF16 = jnp.bfloat16

CHUNK = 64
DIFF_HEADS = 4
DIFF_QK_DIM = 128
DIFF_V_DIM = 2 * DIFF_QK_DIM
DIFF_WIDTH = DIFF_HEADS * DIFF_V_DIM
GMLP_CHUNK = 128
GMLP_GROUPS = 8
GMLP_GROUP_DIM = 128
GMLP_WIDTH = GMLP_GROUPS * GMLP_GROUP_DIM
NUM_BUCKETS = 32
MAX_DISTANCE = 128
EPS = 1e-6
NEG_INF = -1e30
ATTN_SCALE = DIFF_QK_DIM ** -0.5
LOG2_E = math.log2(math.e)
Q_SCALE = ATTN_SCALE * LOG2_E

VMEM_LIMIT_BYTES = 56 * 1024 * 1024
BF16_SUBLANES = 16

IN_TM, IN_TN = 512, 1024
W_CHUNK, W_SLOTS = 128, 3
ATTN_TQ = 256
ATTN_Q_TILES_PER_STEP = 2
MIX_TM = 512
MIX_X_SLOTS = 3
FFN_TM, FFN_TH = 1024, 512
PLE_TM = 512

_NT_DIMS = (((1,), (1,)), ((), ()))


def _dot(a, b):
    return jnp.dot(a, b, preferred_element_type=F32)


def _rms_scale(x):
    return x * lax.rsqrt(jnp.mean(x * x, axis=-1, keepdims=True) + EPS)


def _compiler_params(semantics):
    return pltpu.CompilerParams(dimension_semantics=semantics,
                                vmem_limit_bytes=VMEM_LIMIT_BYTES)


def _in_proj_kernel(x_ref, g_ref, w_hbm, *rest, n_cast):
    w_f32_refs = rest[:n_cast]
    qk_ref, vt_ref, ug_ref = rest[n_cast:n_cast + 3]
    w_bf16_refs = rest[n_cast + 3:2 * n_cast + 3]
    wqk_sc, wvt_sc, wug_sc, stage_sc, sem = rest[2 * n_cast + 3:]
    n_qk, n_v = wqk_sc.shape[1], wvt_sc.shape[0]

    n_slots = stage_sc.shape[0]

    def chunk_copy(c):
        slot = c % n_slots
        return pltpu.make_async_copy(w_hbm.at[c * W_CHUNK:(c + 1) * W_CHUNK, :],
                                     stage_sc.at[slot], sem.at[slot])

    @pl.when(pl.program_id(0) == 0)
    def _():
        n_chunks = w_hbm.shape[0] // W_CHUNK
        for c in range(n_slots - 1):
            chunk_copy(c).start()
        for c in range(n_chunks):
            if c + n_slots - 1 < n_chunks:
                chunk_copy(c + n_slots - 1).start()
            chunk_copy(c).wait()
            rows = slice(c * W_CHUNK, (c + 1) * W_CHUNK)
            blk = stage_sc[c % n_slots]
            wqk_sc[rows, :] = blk[:, :n_qk].astype(BF16)
            wvt_sc[:, rows] = blk[:, n_qk:n_qk + n_v].T.astype(BF16)
            wug_sc[rows, :] = blk[:, n_qk + n_v:].astype(BF16)

    for src, dst in zip(w_f32_refs, w_bf16_refs):
        dst[...] = src[...].astype(BF16)
    a = (_rms_scale(x_ref[...]) * g_ref[...]).astype(BF16)
    for c in range(0, n_qk, IN_TN):
        zc = _dot(a, wqk_sc[:, c:c + IN_TN])
        if c < DIFF_WIDTH:
            zc = zc * Q_SCALE
        qk_ref[:, c:c + IN_TN] = zc.astype(BF16)
    vt = lax.dot_general(wvt_sc[...], a, _NT_DIMS, preferred_element_type=F32).astype(BF16)
    for t in range(vt_ref.shape[0]):
        vt_ref[t] = vt[:, t * ATTN_TQ:(t + 1) * ATTN_TQ]
    for c in range(0, wug_sc.shape[1], IN_TN):
        ug_ref[:, c:c + IN_TN] = jax.nn.gelu(_dot(a, wug_sc[:, c:c + IN_TN])).astype(BF16)


def _slab_specs(weights, n_steps, index_map):
    specs = []
    for w in weights:
        rows, cols = w.shape
        assert rows % (n_steps * BF16_SUBLANES) == 0
        specs.append(pl.BlockSpec((rows // n_steps, cols), index_map))
    return specs


def _in_proj(x2d, g, w_in, bsz, seq, cast_weights):
    m, d = x2d.shape
    tm = IN_TM
    n_qk, n_v, n_ug = 2 * DIFF_WIDTH, DIFF_WIDTH, 2 * GMLP_WIDTH
    assert w_in.shape == (d, n_qk + n_v + n_ug) and d % W_CHUNK == 0 and d // W_CHUNK >= W_SLOTS
    assert seq % tm == 0 and tm % ATTN_TQ == 0
    assert n_qk % IN_TN == 0 and n_ug % IN_TN == 0 and DIFF_WIDTH % IN_TN == 0
    tiles_per_seq = seq // tm
    kt = tm // ATTN_TQ
    slab_specs = _slab_specs(cast_weights, m // tm, lambda i: (i, 0))
    outs = pl.pallas_call(
        functools.partial(_in_proj_kernel, n_cast=len(cast_weights)),
        grid=(m // tm,),
        in_specs=[
            pl.BlockSpec((tm, d), lambda i: (i, 0)),
            pl.BlockSpec((1, d), lambda i: (0, 0)),
            pl.BlockSpec(memory_space=pl.ANY),
        ] + slab_specs,
        out_specs=[
            pl.BlockSpec((tm, n_qk), lambda i: (i, 0)),
            pl.BlockSpec((None, kt, DIFF_WIDTH, ATTN_TQ),
                         lambda i: (i // tiles_per_seq, i % tiles_per_seq, 0, 0)),
            pl.BlockSpec((tm, n_ug), lambda i: (i, 0)),
        ] + slab_specs,
        out_shape=[
            jax.ShapeDtypeStruct((m, n_qk), BF16),
            jax.ShapeDtypeStruct((bsz, seq // ATTN_TQ, DIFF_WIDTH, ATTN_TQ), BF16),
            jax.ShapeDtypeStruct((m, n_ug), BF16),
        ] + [jax.ShapeDtypeStruct(w.shape, BF16) for w in cast_weights],
        scratch_shapes=[
            pltpu.VMEM((d, n_qk), BF16),
            pltpu.VMEM((n_v, d), BF16),
            pltpu.VMEM((d, n_ug), BF16),
            pltpu.VMEM((W_SLOTS, W_CHUNK, w_in.shape[1]), F32),
            pltpu.SemaphoreType.DMA((W_SLOTS,)),
        ],
        compiler_params=_compiler_params(("arbitrary",)),
        name="in_proj",
    )(x2d, g, w_in, *cast_weights)
    return outs[0], outs[1], outs[2], outs[3:]


def _t5_bucket_np(rel):
    half = NUM_BUCKETS // 2
    max_exact = half // 2
    ret = np.where(rel > 0, half, 0)
    n = np.abs(rel)
    nf = np.maximum(n, 1).astype(np.float32)
    large = max_exact + (np.log(nf / np.float32(max_exact))
                         / np.float32(math.log(MAX_DISTANCE / max_exact))
                         * np.float32(half - max_exact)).astype(np.int32)
    large = np.minimum(large, half - 1)
    return (ret + np.where(n < max_exact, n, large)).astype(np.int32)


def _near_bucket_table(seq):
    i = np.arange(ATTN_TQ)[None, :]
    j = np.arange(2 * ATTN_TQ)[:, None] - ATTN_TQ
    visible = np.floor_divide(j, CHUNK) <= i // CHUNK
    table = np.where(visible, _t5_bucket_np(j - i), -1).astype(np.int32)
    far = np.unique(_t5_bucket_np(-np.arange(ATTN_TQ + 1, max(seq, ATTN_TQ + 2))))
    assert far.size == 1, "far keys must share one bias bucket"
    return table, int(far[0])


def _bias_tile_kernel(bucket_ref, rb_ref, out_ref, *, buckets, far_bucket):
    bucket = bucket_ref[...]
    for h in range(DIFF_HEADS):
        shift = rb_ref[far_bucket, h]
        tile = jnp.full(bucket.shape, NEG_INF, F32)
        for b in buckets:
            tile = jnp.where(bucket == b, (rb_ref[b, h] - shift) * LOG2_E, tile)
        out_ref[h] = tile


def _bias_tile(rel_bias, seq):
    table, far_bucket = _near_bucket_table(seq)
    buckets = tuple(int(b) for b in np.unique(table) if b >= 0)
    return pl.pallas_call(
        functools.partial(_bias_tile_kernel, buckets=buckets, far_bucket=far_bucket),
        in_specs=[pl.BlockSpec(memory_space=pltpu.VMEM),
                  pl.BlockSpec(memory_space=pltpu.SMEM)],
        out_specs=pl.BlockSpec(memory_space=pltpu.VMEM),
        out_shape=jax.ShapeDtypeStruct((DIFF_HEADS,) + table.shape, F32),
        name="bias_tile",
    )(jnp.asarray(table), rel_bias)


def _attn_kernel(q_ref, k_ref, vt_ref, bias_ref, lq1_ref, lk1_ref, lq2_ref, lk2_ref,
                 g_ref, *rest, n_q_tiles, n_cast, lam_init):
    w_f32_refs = rest[:n_cast]
    o_ref = rest[n_cast]
    w_bf16_refs = rest[n_cast + 1:2 * n_cast + 1]
    m_sc, l_sc, acc_sc = rest[2 * n_cast + 1:]
    tq = ATTN_TQ
    step = pl.program_id(1)
    chains = [(h, m) for h in range(DIFF_HEADS) for m in range(2)]

    def key_tile(q_rows, kt, bias_rows, init):
        for c, (h, _) in enumerate(chains):
            cols = slice(c * DIFF_QK_DIM, (c + 1) * DIFF_QK_DIM)
            s = lax.dot_general(k_ref[kt * tq:(kt + 1) * tq, cols], q_ref[q_rows, cols], _NT_DIMS,
                                preferred_element_type=F32)
            if bias_rows is not None:
                s = s + bias_ref[h, bias_rows, :]
            m_cur = jnp.max(s, axis=0, keepdims=True)
            if init:
                m_next = m_cur
            else:
                m_prev = m_sc[c]
                m_next = jnp.maximum(m_prev, m_cur)
            p = jnp.exp2(s - m_next)
            p_sum = jnp.sum(p, axis=0, keepdims=True)
            pv = _dot(vt_ref[kt, h * DIFF_V_DIM:(h + 1) * DIFF_V_DIM, :], p.astype(BF16))
            if init:
                l_sc[c] = p_sum
                acc_sc[c] = pv
            else:
                alpha = jnp.exp2(m_prev - m_next)
                l_sc[c] = alpha * l_sc[c] + p_sum
                acc_sc[c] = acc_sc[c] * alpha + pv
            m_sc[c] = m_next

    def query_tile(j, q_rows):
        key_tile(q_rows, j, slice(tq, 2 * tq), init=True)
        if j > 0:
            key_tile(q_rows, j - 1, slice(0, tq), init=False)
        for kt in range(j - 2, -1, -1):
            key_tile(q_rows, kt, None, init=False)

        lam = (jnp.exp(jnp.sum(lq1_ref[...] * lk1_ref[...], axis=-1, keepdims=True))
               - jnp.exp(jnp.sum(lq2_ref[...] * lk2_ref[...], axis=-1, keepdims=True))
               + lam_init)
        for h in range(DIFF_HEADS):
            o = (acc_sc[2 * h] * (1.0 / l_sc[2 * h])
                 - acc_sc[2 * h + 1] * (lam / l_sc[2 * h + 1]))
            r = lax.rsqrt(jnp.mean(o * o, axis=0, keepdims=True) + EPS) * (1.0 - lam_init)
            y = o * r * g_ref[...]
            o_ref[q_rows, h * DIFF_V_DIM:(h + 1) * DIFF_V_DIM] = y.T.astype(BF16)

    def grid_step(s):
        for src, dst in zip(w_f32_refs, w_bf16_refs):
            dst[...] = src[...].astype(BF16)
        for t in range(ATTN_Q_TILES_PER_STEP):
            query_tile(s * ATTN_Q_TILES_PER_STEP + t, slice(t * tq, (t + 1) * tq))

    for s in range(n_q_tiles // ATTN_Q_TILES_PER_STEP):
        pl.when(step == s)(functools.partial(grid_step, s))


def _attention(zqk, vt, bias_tile, lq1, lk1, lq2, lk2, subln_g_col, bsz, seq, lam_init,
               cast_weights):
    tq = ATTN_TQ
    tstep = ATTN_Q_TILES_PER_STEP * tq
    assert seq % tstep == 0
    nq = seq // tq
    ns = seq // tstep
    n_chains = 2 * DIFF_HEADS
    vec = pl.BlockSpec((1, DIFF_QK_DIM), lambda b, qi: (0, 0))
    slab_specs = _slab_specs(cast_weights, bsz * ns, lambda b, qi: (b * ns + qi, 0))
    outs = pl.pallas_call(
        functools.partial(_attn_kernel, n_q_tiles=nq, n_cast=len(cast_weights),
                          lam_init=lam_init),
        grid=(bsz, ns),
        in_specs=[
            pl.BlockSpec((tstep, DIFF_WIDTH), lambda b, qi: (b * ns + qi, 0)),
            pl.BlockSpec((seq, DIFF_WIDTH), lambda b, qi: (b, 1)),
            pl.BlockSpec((None, nq, DIFF_WIDTH, tq), lambda b, qi: (b, 0, 0, 0)),
            pl.BlockSpec((DIFF_HEADS, 2 * tq, tq), lambda b, qi: (0, 0, 0)),
            vec, vec, vec, vec,
            pl.BlockSpec((DIFF_V_DIM, 1), lambda b, qi: (0, 0)),
        ] + slab_specs,
        out_specs=[pl.BlockSpec((tstep, DIFF_WIDTH), lambda b, qi: (b * ns + qi, 0))] + slab_specs,
        out_shape=[jax.ShapeDtypeStruct((bsz * seq, DIFF_WIDTH), BF16)]
        + [jax.ShapeDtypeStruct(w.shape, BF16) for w in cast_weights],
        scratch_shapes=[
            pltpu.VMEM((n_chains, 1, tq), F32),
            pltpu.VMEM((n_chains, 1, tq), F32),
            pltpu.VMEM((n_chains, DIFF_V_DIM, tq), F32),
        ],
        compiler_params=_compiler_params(("arbitrary", "arbitrary")),
        name="diff_attention",
    )(zqk, zqk, vt, bias_tile, lq1, lk1, lq2, lk2, subln_g_col, *cast_weights)
    return outs[0], outs[1:]


def _mix_out_kernel(od_ref, gu_ref, gv_ref, x_hbm, lng_ref, lnb_ref, ws_ref, bs_ref,
                    wo_ref, h1_ref, gm_sc, x_buf, x_sem):
    tm = od_ref.shape[0]
    i = pl.program_id(0)
    n_tiles = pl.num_programs(0)
    n_slots = x_buf.shape[0]

    def x_copy(t):
        slot = t % n_slots
        return pltpu.make_async_copy(x_hbm.at[pl.ds(pl.multiple_of(t * tm, tm), tm), :],
                                     x_buf.at[slot], x_sem.at[slot])

    @pl.when(i == 0)
    def _():
        for t in range(n_slots - 1):
            x_copy(t).start()

    @pl.when(i + n_slots - 1 < n_tiles)
    def _():
        x_copy(i + n_slots - 1).start()

    x_copy(i).wait()
    h1_attn = x_buf[i % n_slots] + _dot(od_ref[...], wo_ref[:DIFF_WIDTH, :])
    gv = gv_ref[...].astype(F32)
    mu = jnp.mean(gv, axis=-1, keepdims=True)
    xc = gv - mu
    vn = (xc * lax.rsqrt(jnp.mean(xc * xc, axis=-1, keepdims=True) + EPS)
          * lng_ref[...] + lnb_ref[...]).astype(BF16)

    row = lax.broadcasted_iota(jnp.int32, (GMLP_CHUNK, GMLP_CHUNK), 0)
    col = lax.broadcasted_iota(jnp.int32, (GMLP_CHUNK, GMLP_CHUNK), 1)
    causal = (col // CHUNK) <= (row // CHUNK)
    for g in range(GMLP_GROUPS):
        ws_g = jnp.where(causal, ws_ref[g], 0.0).astype(BF16)
        cols = slice(g * GMLP_GROUP_DIM, (g + 1) * GMLP_GROUP_DIM)
        for r in range(tm // GMLP_CHUNK):
            rows = slice(r * GMLP_CHUNK, (r + 1) * GMLP_CHUNK)
            s = _dot(ws_g, vn[rows, cols]) + bs_ref[g]
            gm_sc[rows, cols] = (gu_ref[rows, cols].astype(F32) * s).astype(BF16)

    h1_ref[...] = h1_attn + _dot(gm_sc[...], wo_ref[DIFF_WIDTH:, :])


def _mix_out(o_diff, zug, x2d, ln_g, ln_b, ws, bs, w_out_bf16):
    m, d = x2d.shape
    tm = MIX_TM
    assert m % tm == 0 and tm % GMLP_CHUNK == 0 and m // tm >= MIX_X_SLOTS
    full = lambda shape: pl.BlockSpec(shape, lambda i: (0,) * len(shape))
    return pl.pallas_call(
        _mix_out_kernel,
        grid=(m // tm,),
        in_specs=[
            pl.BlockSpec((tm, DIFF_WIDTH), lambda i: (i, 0)),
            pl.BlockSpec((tm, GMLP_WIDTH), lambda i: (i, 0)),
            pl.BlockSpec((tm, GMLP_WIDTH), lambda i: (i, 1)),
            pl.BlockSpec(memory_space=pl.ANY),
            full((1, GMLP_WIDTH)), full((1, GMLP_WIDTH)),
            full((GMLP_GROUPS, GMLP_CHUNK, GMLP_CHUNK)),
            full((GMLP_GROUPS, GMLP_CHUNK, 1)),
            full((DIFF_WIDTH + GMLP_WIDTH, d)),
        ],
        out_specs=pl.BlockSpec((tm, d), lambda i: (i, 0)),
        out_shape=jax.ShapeDtypeStruct((m, d), F32),
        scratch_shapes=[pltpu.VMEM((tm, GMLP_WIDTH), BF16),
                        pltpu.VMEM((MIX_X_SLOTS, tm, d), F32),
                        pltpu.SemaphoreType.DMA((MIX_X_SLOTS,))],
        compiler_params=_compiler_params(("arbitrary",)),
        name="mix_out",
    )(o_diff, zug, zug, x2d, ln_g, ln_b, ws, bs, w_out_bf16)


def _ffn_kernel(h1_ref, fg_ref, w1_ref, w3_ref, w2_ref, o_ref, f_sc):
    k = pl.program_id(1)

    def hidden_tile(f, base):
        act = (jax.nn.silu(_dot(f, w1_ref[...])) * _dot(f, w3_ref[...])).astype(BF16)
        o_ref[...] = base + _dot(act, w2_ref[...])

    @pl.when(k == 0)
    def _():
        h1 = h1_ref[...]
        f = (_rms_scale(h1) * fg_ref[...]).astype(BF16)
        f_sc[...] = f
        hidden_tile(f, h1)

    @pl.when(k > 0)
    def _():
        hidden_tile(f_sc[...], o_ref[...])


def _ffn(h1, ffn_g, w1, w3, w2):
    m, d = h1.shape
    hid = w1.shape[1]
    tm, th = FFN_TM, FFN_TH
    assert m % tm == 0 and hid % th == 0
    return pl.pallas_call(
        _ffn_kernel,
        grid=(m // tm, hid // th),
        in_specs=[
            pl.BlockSpec((tm, d), lambda i, k: (i, 0)),
            pl.BlockSpec((1, d), lambda i, k: (0, 0)),
            pl.BlockSpec((d, th), lambda i, k: (0, k)),
            pl.BlockSpec((d, th), lambda i, k: (0, k)),
            pl.BlockSpec((th, d), lambda i, k: (k, 0)),
        ],
        out_specs=pl.BlockSpec((tm, d), lambda i, k: (i, 0)),
        out_shape=jax.ShapeDtypeStruct((m, d), F32),
        scratch_shapes=[pltpu.VMEM((tm, d), BF16)],
        compiler_params=_compiler_params(("arbitrary", "arbitrary")),
        name="ffn",
    )(h1, ffn_g, w1, w3, w2)


def _ple_kernel(h2_ref, p_ref, wg_ref, wu_ref, g_ref, o_ref, *, final_norm):
    h2 = h2_ref[...]
    gate = _dot(h2.astype(BF16), wg_ref[...])
    up = _dot(p_ref[...].astype(BF16), wu_ref[...])
    h3 = h2 + up * jax.nn.sigmoid(gate)
    if final_norm:
        h3 = _rms_scale(h3) * g_ref[...]
    o_ref[...] = h3


def _ple(h2, p2d, w_gate, w_up, final_g, final_norm):
    m, d = h2.shape
    pd = p2d.shape[1]
    tm = PLE_TM
    assert m % tm == 0
    full = lambda shape: pl.BlockSpec(shape, lambda i: (0,) * len(shape))
    return pl.pallas_call(
        functools.partial(_ple_kernel, final_norm=final_norm),
        grid=(m // tm,),
        in_specs=[
            pl.BlockSpec((tm, d), lambda i: (i, 0)),
            pl.BlockSpec((tm, pd), lambda i: (i, 0)),
            full((d, d)), full((pd, d)), full((1, d)),
        ],
        out_specs=pl.BlockSpec((tm, d), lambda i: (i, 0)),
        out_shape=jax.ShapeDtypeStruct((m, d), F32),
        compiler_params=_compiler_params(("arbitrary",)),
        name="ple",
    )(h2, p2d, w_gate, w_up, final_g)


def kernel(x, p, w_in, w_out, attn_norm_g, ffn_norm_g, final_norm_g, lambda_q1, lambda_k1,
           lambda_q2, lambda_k2, subln_g, rel_bias, gmlp_ln_g, gmlp_ln_b, gmlp_ws, gmlp_b,
           ffn_w1, ffn_w3, ffn_w2, pl_w_up, pl_w_gate):
    bsz, seq, d = x.shape
    depth = w_in.shape[0]
    qk_cols = DIFF_HEADS * 2 * DIFF_QK_DIM
    assert w_in.shape[2] == 2 * qk_cols + DIFF_WIDTH + 2 * GMLP_WIDTH
    assert seq % GMLP_CHUNK == 0
    row = lambda v: v.reshape(1, -1)

    bias_tile = _bias_tile(rel_bias, seq)
    h = x.reshape(bsz * seq, d)
    for i in range(depth):
        lam_init = 0.8 - 0.6 * math.exp(-0.3 * i)
        zqk, vt, zug, (w2,) = _in_proj(h, row(attn_norm_g[i]), w_in[i], bsz, seq,
                                       cast_weights=(ffn_w2[i],))
        o_diff, (w1, w3, wo, wg) = _attention(
            zqk, vt, bias_tile, row(lambda_q1[i]), row(lambda_k1[i]), row(lambda_q2[i]),
            row(lambda_k2[i]), subln_g[i].reshape(-1, 1), bsz, seq, lam_init,
            cast_weights=(ffn_w1[i], ffn_w3[i], w_out[i], pl_w_gate[i]))
        h1 = _mix_out(o_diff, zug, h, row(gmlp_ln_g[i]), row(gmlp_ln_b[i]), gmlp_ws[i],
                      gmlp_b[i].reshape(GMLP_GROUPS, GMLP_CHUNK, 1), wo)
        h2 = _ffn(h1, row(ffn_norm_g[i]), w1, w3, w2)
        h = _ple(h2, p[i].reshape(bsz * seq, -1), wg, pl_w_up[i].astype(BF16),
                 row(final_norm_g), final_norm=(i == depth - 1))
    return h.reshape(bsz, seq, d)
```

```python
import functools
import math

import numpy as np
import jax
import jax.numpy as jnp
from jax import lax
from jax.experimental import pallas as pl
from jax.experimental.pallas import tpu as pltpu

F32 = jnp.float32
BF16 = jnp.bfloat16

CHUNK = 64
DIFF_HEADS = 4
DIFF_QK_DIM = 128
DIFF_V_DIM = 2 * DIFF_QK_DIM
DIFF_WIDTH = DIFF_HEADS * DIFF_V_DIM
GMLP_CHUNK = 128
GMLP_GROUPS = 8
GMLP_GROUP_DIM = 128
GMLP_WIDTH = GMLP_GROUPS * GMLP_GROUP_DIM
NUM_BUCKETS = 32
MAX_DISTANCE = 128
EPS = 1e-6
NEG_INF = -1e30
ATTN_SCALE = DIFF_QK_DIM ** -0.5
LOG2_E = math.log2(math.e)
Q_SCALE = ATTN_SCALE * LOG2_E

VMEM_LIMIT_BYTES = 56 * 1024 * 1024
BF16_SUBLANES = 16

IN_TM, IN_TN = 512, 1024
W_CHUNK, W_SLOTS = 128, 3
ATTN_TQ = 256
ATTN_Q_TILES_PER_STEP = 4
MIX_TM = 512
FFN_TM, FFN_TH = 1024, 512
PLE_TM = 512

_NT_DIMS = (((1,), (1,)), ((), ()))


def _dot(a, b):
    return jnp.dot(a, b, preferred_element_type=F32)


def _rms_scale(x):
    return x * lax.rsqrt(jnp.mean(x * x, axis=-1, keepdims=True) + EPS)


def _compiler_params(semantics):
    return pltpu.CompilerParams(dimension_semantics=semantics,
                                vmem_limit_bytes=VMEM_LIMIT_BYTES)


def _in_proj_kernel(x_ref, g_ref, w_hbm, *rest, n_cast):
    w_f32_refs = rest[:n_cast]
    qk_ref, vt_ref, ug_ref = rest[n_cast:n_cast + 3]
    w_bf16_refs = rest[n_cast + 3:2 * n_cast + 3]
    wqk_sc, wvt_sc, wug_sc, stage_sc, sem = rest[2 * n_cast + 3:]
    n_qk, n_v = wqk_sc.shape[1], wvt_sc.shape[0]

    n_slots = stage_sc.shape[0]

    def chunk_copy(c):
        slot = c % n_slots
        return pltpu.make_async_copy(w_hbm.at[c * W_CHUNK:(c + 1) * W_CHUNK, :],
                                     stage_sc.at[slot], sem.at[slot])

    @pl.when(pl.program_id(0) == 0)
    def _():
        n_chunks = w_hbm.shape[0] // W_CHUNK
        for c in range(n_slots - 1):
            chunk_copy(c).start()
        for c in range(n_chunks):
            if c + n_slots - 1 < n_chunks:
                chunk_copy(c + n_slots - 1).start()
            chunk_copy(c).wait()
            rows = slice(c * W_CHUNK, (c + 1) * W_CHUNK)
            blk = stage_sc[c % n_slots]
            wqk_sc[rows, :] = blk[:, :n_qk].astype(BF16)
            wvt_sc[:, rows] = blk[:, n_qk:n_qk + n_v].T.astype(BF16)
            wug_sc[rows, :] = blk[:, n_qk + n_v:].astype(BF16)

    for src, dst in zip(w_f32_refs, w_bf16_refs):
        dst[...] = src[...].astype(BF16)
    a = (_rms_scale(x_ref[...]) * g_ref[...]).astype(BF16)
    for c in range(0, n_qk, IN_TN):
        zc = _dot(a, wqk_sc[:, c:c + IN_TN])
        if c < DIFF_WIDTH:
            zc = zc * Q_SCALE
        qk_ref[:, c:c + IN_TN] = zc.astype(BF16)
    vt = lax.dot_general(wvt_sc[...], a, _NT_DIMS, preferred_element_type=F32).astype(BF16)
    for t in range(vt_ref.shape[0]):
        vt_ref[t] = vt[:, t * ATTN_TQ:(t + 1) * ATTN_TQ]
    for c in range(0, wug_sc.shape[1], IN_TN):
        ug_ref[:, c:c + IN_TN] = jax.nn.gelu(_dot(a, wug_sc[:, c:c + IN_TN])).astype(BF16)


def _slab_specs(weights, n_steps, index_map):
    specs = []
    for w in weights:
        rows, cols = w.shape
        assert rows % (n_steps * BF16_SUBLANES) == 0
        specs.append(pl.BlockSpec((rows // n_steps, cols), index_map))
    return specs


def _in_proj(x2d, g, w_in, bsz, seq, cast_weights):
    m, d = x2d.shape
    tm = IN_TM
    n_qk, n_v, n_ug = 2 * DIFF_WIDTH, DIFF_WIDTH, 2 * GMLP_WIDTH
    assert w_in.shape == (d, n_qk + n_v + n_ug) and d % W_CHUNK == 0 and d // W_CHUNK >= W_SLOTS
    assert seq % tm == 0 and tm % ATTN_TQ == 0
    assert n_qk % IN_TN == 0 and n_ug % IN_TN == 0 and DIFF_WIDTH % IN_TN == 0
    tiles_per_seq = seq // tm
    kt = tm // ATTN_TQ
    slab_specs = _slab_specs(cast_weights, m // tm, lambda i: (i, 0))
    outs = pl.pallas_call(
        functools.partial(_in_proj_kernel, n_cast=len(cast_weights)),
        grid=(m // tm,),
        in_specs=[
            pl.BlockSpec((tm, d), lambda i: (i, 0)),
            pl.BlockSpec((1, d), lambda i: (0, 0)),
            pl.BlockSpec(memory_space=pl.ANY),
        ] + slab_specs,
        out_specs=[
            pl.BlockSpec((tm, n_qk), lambda i: (i, 0)),
            pl.BlockSpec((None, kt, DIFF_WIDTH, ATTN_TQ),
                         lambda i: (i // tiles_per_seq, i % tiles_per_seq, 0, 0)),
            pl.BlockSpec((tm, n_ug), lambda i: (i, 0)),
        ] + slab_specs,
        out_shape=[
            jax.ShapeDtypeStruct((m, n_qk), BF16),
            jax.ShapeDtypeStruct((bsz, seq // ATTN_TQ, DIFF_WIDTH, ATTN_TQ), BF16),
            jax.ShapeDtypeStruct((m, n_ug), BF16),
        ] + [jax.ShapeDtypeStruct(w.shape, BF16) for w in cast_weights],
        scratch_shapes=[
            pltpu.VMEM((d, n_qk), BF16),
            pltpu.VMEM((n_v, d), BF16),
            pltpu.VMEM((d, n_ug), BF16),
            pltpu.VMEM((W_SLOTS, W_CHUNK, w_in.shape[1]), F32),
            pltpu.SemaphoreType.DMA((W_SLOTS,)),
        ],
        compiler_params=_compiler_params(("arbitrary",)),
        name="in_proj",
    )(x2d, g, w_in, *cast_weights)
    return outs[0], outs[1], outs[2], outs[3:]


def _t5_bucket_np(rel):
    half = NUM_BUCKETS // 2
    max_exact = half // 2
    ret = np.where(rel > 0, half, 0)
    n = np.abs(rel)
    nf = np.maximum(n, 1).astype(np.float32)
    large = max_exact + (np.log(nf / np.float32(max_exact))
                         / np.float32(math.log(MAX_DISTANCE / max_exact))
                         * np.float32(half - max_exact)).astype(np.int32)
    large = np.minimum(large, half - 1)
    return (ret + np.where(n < max_exact, n, large)).astype(np.int32)


def _near_bucket_table(seq):
    i = np.arange(ATTN_TQ)[None, :]
    j = np.arange(2 * ATTN_TQ)[:, None] - ATTN_TQ
    visible = np.floor_divide(j, CHUNK) <= i // CHUNK
    table = np.where(visible, _t5_bucket_np(j - i), -1).astype(np.int32)
    far = np.unique(_t5_bucket_np(-np.arange(ATTN_TQ + 1, max(seq, ATTN_TQ + 2))))
    assert far.size == 1, "far keys must share one bias bucket"
    return table, int(far[0])


def _bias_tile_kernel(bucket_ref, rb_ref, out_ref, *, buckets, far_bucket):
    bucket = bucket_ref[...]
    for h in range(DIFF_HEADS):
        shift = rb_ref[far_bucket, h]
        tile = jnp.full(bucket.shape, NEG_INF, F32)
        for b in buckets:
            tile = jnp.where(bucket == b, (rb_ref[b, h] - shift) * LOG2_E, tile)
        out_ref[h] = tile


def _bias_tile(rel_bias, seq):
    table, far_bucket = _near_bucket_table(seq)
    buckets = tuple(int(b) for b in np.unique(table) if b >= 0)
    return pl.pallas_call(
        functools.partial(_bias_tile_kernel, buckets=buckets, far_bucket=far_bucket),
        in_specs=[pl.BlockSpec(memory_space=pltpu.VMEM),
                  pl.BlockSpec(memory_space=pltpu.SMEM)],
        out_specs=pl.BlockSpec(memory_space=pltpu.VMEM),
        out_shape=jax.ShapeDtypeStruct((DIFF_HEADS,) + table.shape, F32),
        name="bias_tile",
    )(jnp.asarray(table), rel_bias)


def _attn_kernel(q_ref, k_ref, vt_ref, bias_ref, lq1_ref, lk1_ref, lq2_ref, lk2_ref,
                 g_ref, *rest, n_q_tiles, n_cast, lam_init):
    w_f32_refs = rest[:n_cast]
    o_ref = rest[n_cast]
    w_bf16_refs = rest[n_cast + 1:2 * n_cast + 1]
    m_sc, l_sc, acc_sc = rest[2 * n_cast + 1:]
    tq = ATTN_TQ
    step = pl.program_id(1)
    chains = [(h, m) for h in range(DIFF_HEADS) for m in range(2)]

    def key_tile(q_rows, kt, bias_rows, init):
        for c, (h, _) in enumerate(chains):
            cols = slice(c * DIFF_QK_DIM, (c + 1) * DIFF_QK_DIM)
            s = lax.dot_general(k_ref[kt * tq:(kt + 1) * tq, cols], q_ref[q_rows, cols], _NT_DIMS,
                                preferred_element_type=F32)
            if bias_rows is not None:
                s = s + bias_ref[h, bias_rows, :]
            m_cur = jnp.max(s, axis=0, keepdims=True)
            if init:
                m_next = m_cur
            else:
                m_prev = m_sc[c]
                m_next = jnp.maximum(m_prev, m_cur)
            p = jnp.exp2(s - m_next)
            p_sum = jnp.sum(p, axis=0, keepdims=True)
            pv = _dot(vt_ref[kt, h * DIFF_V_DIM:(h + 1) * DIFF_V_DIM, :], p.astype(BF16))
            if init:
                l_sc[c] = p_sum
                acc_sc[c] = pv
            else:
                alpha = jnp.exp2(m_prev - m_next)
                l_sc[c] = alpha * l_sc[c] + p_sum
                acc_sc[c] = acc_sc[c] * alpha + pv
            m_sc[c] = m_next

    def query_tile(j, q_rows):
        key_tile(q_rows, j, slice(tq, 2 * tq), init=True)
        if j > 0:
            key_tile(q_rows, j - 1, slice(0, tq), init=False)
        for kt in range(j - 2, -1, -1):
            key_tile(q_rows, kt, None, init=False)

        lam = (jnp.exp(jnp.sum(lq1_ref[...] * lk1_ref[...], axis=-1, keepdims=True))
               - jnp.exp(jnp.sum(lq2_ref[...] * lk2_ref[...], axis=-1, keepdims=True))
               + lam_init)
        for h in range(DIFF_HEADS):
            o = (acc_sc[2 * h] * (1.0 / l_sc[2 * h])
                 - acc_sc[2 * h + 1] * (lam / l_sc[2 * h + 1]))
            r = lax.rsqrt(jnp.mean(o * o, axis=0, keepdims=True) + EPS) * (1.0 - lam_init)
            y = o * r * g_ref[...]
            o_ref[q_rows, h * DIFF_V_DIM:(h + 1) * DIFF_V_DIM] = y.T.astype(BF16)

    def grid_step(s):
        for src, dst in zip(w_f32_refs, w_bf16_refs):
            dst[...] = src[...].astype(BF16)
        for t in range(ATTN_Q_TILES_PER_STEP):
            query_tile(s * ATTN_Q_TILES_PER_STEP + t, slice(t * tq, (t + 1) * tq))

    for s in range(n_q_tiles // ATTN_Q_TILES_PER_STEP):
        pl.when(step == s)(functools.partial(grid_step, s))


def _attention(zqk, vt, bias_tile, lq1, lk1, lq2, lk2, subln_g_col, bsz, seq, lam_init,
               cast_weights):
    tq = ATTN_TQ
    tstep = ATTN_Q_TILES_PER_STEP * tq
    assert seq % tstep == 0
    nq = seq // tq
    ns = seq // tstep
    n_chains = 2 * DIFF_HEADS
    vec = pl.BlockSpec((1, DIFF_QK_DIM), lambda b, qi: (0, 0))
    slab_specs = _slab_specs(cast_weights, bsz * ns, lambda b, qi: (b * ns + qi, 0))
    outs = pl.pallas_call(
        functools.partial(_attn_kernel, n_q_tiles=nq, n_cast=len(cast_weights),
                          lam_init=lam_init),
        grid=(bsz, ns),
        in_specs=[
            pl.BlockSpec((tstep, DIFF_WIDTH), lambda b, qi: (b * ns + qi, 0)),
            pl.BlockSpec((seq, DIFF_WIDTH), lambda b, qi: (b, 1)),
            pl.BlockSpec((None, nq, DIFF_WIDTH, tq), lambda b, qi: (b, 0, 0, 0)),
            pl.BlockSpec((DIFF_HEADS, 2 * tq, tq), lambda b, qi: (0, 0, 0)),
            vec, vec, vec, vec,
            pl.BlockSpec((DIFF_V_DIM, 1), lambda b, qi: (0, 0)),
        ] + slab_specs,
        out_specs=[pl.BlockSpec((tstep, DIFF_WIDTH), lambda b, qi: (b * ns + qi, 0))] + slab_specs,
        out_shape=[jax.ShapeDtypeStruct((bsz * seq, DIFF_WIDTH), BF16)]
        + [jax.ShapeDtypeStruct(w.shape, BF16) for w in cast_weights],
        scratch_shapes=[
            pltpu.VMEM((n_chains, 1, tq), F32),
            pltpu.VMEM((n_chains, 1, tq), F32),
            pltpu.VMEM((n_chains, DIFF_V_DIM, tq), F32),
        ],
        compiler_params=_compiler_params(("arbitrary", "arbitrary")),
        name="diff_attention",
    )(zqk, zqk, vt, bias_tile, lq1, lk1, lq2, lk2, subln_g_col, *cast_weights)
    return outs[0], outs[1:]


def _mix_out_kernel(od_ref, gu_ref, gv_ref, x_ref, lng_ref, lnb_ref, ws_ref, bs_ref,
                    wo_ref, h1_ref, gm_sc):
    tm = od_ref.shape[0]
    h1_attn = x_ref[...] + _dot(od_ref[...], wo_ref[:DIFF_WIDTH, :])
    gv = gv_ref[...].astype(F32)
    mu = jnp.mean(gv, axis=-1, keepdims=True)
    xc = gv - mu
    vn = (xc * lax.rsqrt(jnp.mean(xc * xc, axis=-1, keepdims=True) + EPS)
          * lng_ref[...] + lnb_ref[...]).astype(BF16)

    row = lax.broadcasted_iota(jnp.int32, (GMLP_CHUNK, GMLP_CHUNK), 0)
    col = lax.broadcasted_iota(jnp.int32, (GMLP_CHUNK, GMLP_CHUNK), 1)
    causal = (col // CHUNK) <= (row // CHUNK)
    for g in range(GMLP_GROUPS):
        ws_g = jnp.where(causal, ws_ref[g], 0.0).astype(BF16)
        cols = slice(g * GMLP_GROUP_DIM, (g + 1) * GMLP_GROUP_DIM)
        for r in range(tm // GMLP_CHUNK):
            rows = slice(r * GMLP_CHUNK, (r + 1) * GMLP_CHUNK)
            s = _dot(ws_g, vn[rows, cols]) + bs_ref[g]
            gm_sc[rows, cols] = (gu_ref[rows, cols].astype(F32) * s).astype(BF16)

    h1_ref[...] = h1_attn + _dot(gm_sc[...], wo_ref[DIFF_WIDTH:, :])


def _mix_out(o_diff, zug, x2d, ln_g, ln_b, ws, bs, w_out_bf16):
    m, d = x2d.shape
    tm = MIX_TM
    assert m % tm == 0 and tm % GMLP_CHUNK == 0
    full = lambda shape: pl.BlockSpec(shape, lambda i: (0,) * len(shape))
    return pl.pallas_call(
        _mix_out_kernel,
        grid=(m // tm,),
        in_specs=[
            pl.BlockSpec((tm, DIFF_WIDTH), lambda i: (i, 0)),
            pl.BlockSpec((tm, GMLP_WIDTH), lambda i: (i, 0)),
            pl.BlockSpec((tm, GMLP_WIDTH), lambda i: (i, 1)),
            pl.BlockSpec((tm, d), lambda i: (i, 0)),
            full((1, GMLP_WIDTH)), full((1, GMLP_WIDTH)),
            full((GMLP_GROUPS, GMLP_CHUNK, GMLP_CHUNK)),
            full((GMLP_GROUPS, GMLP_CHUNK, 1)),
            full((DIFF_WIDTH + GMLP_WIDTH, d)),
        ],
        out_specs=pl.BlockSpec((tm, d), lambda i: (i, 0)),
        out_shape=jax.ShapeDtypeStruct((m, d), F32),
        scratch_shapes=[pltpu.VMEM((tm, GMLP_WIDTH), BF16)],
        compiler_params=_compiler_params(("arbitrary",)),
        name="mix_out",
    )(o_diff, zug, zug, x2d, ln_g, ln_b, ws, bs, w_out_bf16)


def _ffn_kernel(h1_ref, fg_ref, w1_ref, w3_ref, w2_ref, o_ref, f_sc):
    k = pl.program_id(1)

    def hidden_tile(f, base):
        act = (jax.nn.silu(_dot(f, w1_ref[...])) * _dot(f, w3_ref[...])).astype(BF16)
        o_ref[...] = base + _dot(act, w2_ref[...])

    @pl.when(k == 0)
    def _():
        h1 = h1_ref[...]
        f = (_rms_scale(h1) * fg_ref[...]).astype(BF16)
        f_sc[...] = f
        hidden_tile(f, h1)

    @pl.when(k > 0)
    def _():
        hidden_tile(f_sc[...], o_ref[...])


def _ffn(h1, ffn_g, w1, w3, w2):
    m, d = h1.shape
    hid = w1.shape[1]
    tm, th = FFN_TM, FFN_TH
    assert m % tm == 0 and hid % th == 0
    return pl.pallas_call(
        _ffn_kernel,
        grid=(m // tm, hid // th),
        in_specs=[
            pl.BlockSpec((tm, d), lambda i, k: (i, 0)),
            pl.BlockSpec((1, d), lambda i, k: (0, 0)),
            pl.BlockSpec((d, th), lambda i, k: (0, k)),
            pl.BlockSpec((d, th), lambda i, k: (0, k)),
            pl.BlockSpec((th, d), lambda i, k: (k, 0)),
        ],
        out_specs=pl.BlockSpec((tm, d), lambda i, k: (i, 0)),
        out_shape=jax.ShapeDtypeStruct((m, d), F32),
        scratch_shapes=[pltpu.VMEM((tm, d), BF16)],
        compiler_params=_compiler_params(("arbitrary", "arbitrary")),
        name="ffn",
    )(h1, ffn_g, w1, w3, w2)


def _ple_kernel(h2_ref, p_ref, wg_ref, wu_ref, g_ref, o_ref, *, final_norm):
    h2 = h2_ref[...]
    gate = _dot(h2.astype(BF16), wg_ref[...])
    up = _dot(p_ref[...].astype(BF16), wu_ref[...])
    h3 = h2 + up * jax.nn.sigmoid(gate)
    if final_norm:
        h3 = _rms_scale(h3) * g_ref[...]
    o_ref[...] = h3


def _ple(h2, p2d, w_gate, w_up, final_g, final_norm):
    m, d = h2.shape
    pd = p2d.shape[1]
    tm = PLE_TM
    assert m % tm == 0
    full = lambda shape: pl.BlockSpec(shape, lambda i: (0,) * len(shape))
    return pl.pallas_call(
        functools.partial(_ple_kernel, final_norm=final_norm),
        grid=(m // tm,),
        in_specs=[
            pl.BlockSpec((tm, d), lambda i: (i, 0)),
            pl.BlockSpec((tm, pd), lambda i: (i, 0)),
            full((d, d)), full((pd, d)), full((1, d)),
        ],
        out_specs=pl.BlockSpec((tm, d), lambda i: (i, 0)),
        out_shape=jax.ShapeDtypeStruct((m, d), F32),
        compiler_params=_compiler_params(("arbitrary",)),
        name="ple",
    )(h2, p2d, w_gate, w_up, final_g)


def kernel(x, p, w_in, w_out, attn_norm_g, ffn_norm_g, final_norm_g, lambda_q1, lambda_k1,
           lambda_q2, lambda_k2, subln_g, rel_bias, gmlp_ln_g, gmlp_ln_b, gmlp_ws, gmlp_b,
           ffn_w1, ffn_w3, ffn_w2, pl_w_up, pl_w_gate):
    bsz, seq, d = x.shape
    depth = w_in.shape[0]
    qk_cols = DIFF_HEADS * 2 * DIFF_QK_DIM
    assert w_in.shape[2] == 2 * qk_cols + DIFF_WIDTH + 2 * GMLP_WIDTH
    assert seq % GMLP_CHUNK == 0
    row = lambda v: v.reshape(1, -1)

    bias_tile = _bias_tile(rel_bias, seq)
    h = x.reshape(bsz * seq, d)
    for i in range(depth):
        lam_init = 0.8 - 0.6 * math.exp(-0.3 * i)
        zqk, vt, zug, (w2, wo, wg) = _in_proj(h, row(attn_norm_g[i]), w_in[i], bsz, seq,
                                              cast_weights=(ffn_w2[i], w_out[i], pl_w_gate[i]))
        o_diff, (w1, w3) = _attention(
            zqk, vt, bias_tile, row(lambda_q1[i]), row(lambda_k1[i]), row(lambda_q2[i]),
            row(lambda_k2[i]), subln_g[i].reshape(-1, 1), bsz, seq, lam_init,
            cast_weights=(ffn_w1[i], ffn_w3[i]))
        h1 = _mix_out(o_diff, zug, h, row(gmlp_ln_g[i]), row(gmlp_ln_b[i]), gmlp_ws[i],
                      gmlp_b[i].reshape(GMLP_GROUPS, GMLP_CHUNK, 1), wo)
        h2 = _ffn(h1, row(ffn_norm_g[i]), w1, w3, w2)
        h = _ple(h2, p[i].reshape(bsz * seq, -1), wg, pl_w_up[i].astype(BF16),
                 row(final_norm_g), final_norm=(i == depth - 1))
    return h.reshape(bsz, seq, d)
```

```python
import functools
import math

import numpy as np
import jax
import jax.numpy as jnp
from jax import lax
from jax.experimental import pallas as pl
from jax.experimental.pallas import tpu as pltpu

F32 = jnp.float32
BF16 = jnp.bfloat16

CHUNK = 64
DIFF_HEADS = 4
DIFF_QK_DIM = 128
DIFF_V_DIM = 2 * DIFF_QK_DIM
DIFF_WIDTH = DIFF_HEADS * DIFF_V_DIM
GMLP_CHUNK = 128
GMLP_GROUPS = 8
GMLP_GROUP_DIM = 128
GMLP_WIDTH = GMLP_GROUPS * GMLP_GROUP_DIM
NUM_BUCKETS = 32
MAX_DISTANCE = 128
EPS = 1e-6
NEG_INF = -1e30
ATTN_SCALE = DIFF_QK_DIM ** -0.5
LOG2_E = math.log2(math.e)
Q_SCALE = ATTN_SCALE * LOG2_E

VMEM_LIMIT_BYTES = 56 * 1024 * 1024
BF16_SUBLANES = 16

IN_TM, IN_TN = 512, 1024
W_CHUNK, W_SLOTS = 128, 3
ATTN_TQ = 256
MIX_TM = 512
FFN_TM, FFN_TH = 1024, 512
PLE_TM = 512

_NT_DIMS = (((1,), (1,)), ((), ()))


def _dot(a, b):
    return jnp.dot(a, b, preferred_element_type=F32)


def _rms_scale(x):
    return x * lax.rsqrt(jnp.mean(x * x, axis=-1, keepdims=True) + EPS)


def _compiler_params(semantics):
    return pltpu.CompilerParams(dimension_semantics=semantics,
                                vmem_limit_bytes=VMEM_LIMIT_BYTES)


def _in_proj_kernel(x_ref, g_ref, w_hbm, *rest, n_cast):
    w_f32_refs = rest[:n_cast]
    qk_ref, vt_ref, ug_ref = rest[n_cast:n_cast + 3]
    w_bf16_refs = rest[n_cast + 3:2 * n_cast + 3]
    wqk_sc, wvt_sc, wug_sc, stage_sc, sem = rest[2 * n_cast + 3:]
    n_qk, n_v = wqk_sc.shape[1], wvt_sc.shape[0]

    n_slots = stage_sc.shape[0]

    def chunk_copy(c):
        slot = c % n_slots
        return pltpu.make_async_copy(w_hbm.at[c * W_CHUNK:(c + 1) * W_CHUNK, :],
                                     stage_sc.at[slot], sem.at[slot])

    @pl.when(pl.program_id(0) == 0)
    def _():
        n_chunks = w_hbm.shape[0] // W_CHUNK
        for c in range(n_slots - 1):
            chunk_copy(c).start()
        for c in range(n_chunks):
            if c + n_slots - 1 < n_chunks:
                chunk_copy(c + n_slots - 1).start()
            chunk_copy(c).wait()
            rows = slice(c * W_CHUNK, (c + 1) * W_CHUNK)
            blk = stage_sc[c % n_slots]
            wqk_sc[rows, :] = blk[:, :n_qk].astype(BF16)
            wvt_sc[:, rows] = blk[:, n_qk:n_qk + n_v].T.astype(BF16)
            wug_sc[rows, :] = blk[:, n_qk + n_v:].astype(BF16)

    for src, dst in zip(w_f32_refs, w_bf16_refs):
        dst[...] = src[...].astype(BF16)
    a = (_rms_scale(x_ref[...]) * g_ref[...]).astype(BF16)
    for c in range(0, n_qk, IN_TN):
        zc = _dot(a, wqk_sc[:, c:c + IN_TN])
        if c < DIFF_WIDTH:
            zc = zc * Q_SCALE
        qk_ref[:, c:c + IN_TN] = zc.astype(BF16)
    vt = lax.dot_general(wvt_sc[...], a, _NT_DIMS, preferred_element_type=F32).astype(BF16)
    for t in range(vt_ref.shape[0]):
        vt_ref[t] = vt[:, t * ATTN_TQ:(t + 1) * ATTN_TQ]
    for c in range(0, wug_sc.shape[1], IN_TN):
        ug_ref[:, c:c + IN_TN] = jax.nn.gelu(_dot(a, wug_sc[:, c:c + IN_TN])).astype(BF16)


def _slab_specs(weights, n_steps, index_map):
    specs = []
    for w in weights:
        rows, cols = w.shape
        assert rows % (n_steps * BF16_SUBLANES) == 0
        specs.append(pl.BlockSpec((rows // n_steps, cols), index_map))
    return specs


def _in_proj(x2d, g, w_in, bsz, seq, cast_weights):
    m, d = x2d.shape
    tm = IN_TM
    n_qk, n_v, n_ug = 2 * DIFF_WIDTH, DIFF_WIDTH, 2 * GMLP_WIDTH
    assert w_in.shape == (d, n_qk + n_v + n_ug) and d % W_CHUNK == 0 and d // W_CHUNK >= W_SLOTS
    assert seq % tm == 0 and tm % ATTN_TQ == 0
    assert n_qk % IN_TN == 0 and n_ug % IN_TN == 0 and DIFF_WIDTH % IN_TN == 0
    tiles_per_seq = seq // tm
    kt = tm // ATTN_TQ
    slab_specs = _slab_specs(cast_weights, m // tm, lambda i: (i, 0))
    outs = pl.pallas_call(
        functools.partial(_in_proj_kernel, n_cast=len(cast_weights)),
        grid=(m // tm,),
        in_specs=[
            pl.BlockSpec((tm, d), lambda i: (i, 0)),
            pl.BlockSpec((1, d), lambda i: (0, 0)),
            pl.BlockSpec(memory_space=pl.ANY),
        ] + slab_specs,
        out_specs=[
            pl.BlockSpec((tm, n_qk), lambda i: (i, 0)),
            pl.BlockSpec((None, kt, DIFF_WIDTH, ATTN_TQ),
                         lambda i: (i // tiles_per_seq, i % tiles_per_seq, 0, 0)),
            pl.BlockSpec((tm, n_ug), lambda i: (i, 0)),
        ] + slab_specs,
        out_shape=[
            jax.ShapeDtypeStruct((m, n_qk), BF16),
            jax.ShapeDtypeStruct((bsz, seq // ATTN_TQ, DIFF_WIDTH, ATTN_TQ), BF16),
            jax.ShapeDtypeStruct((m, n_ug), BF16),
        ] + [jax.ShapeDtypeStruct(w.shape, BF16) for w in cast_weights],
        scratch_shapes=[
            pltpu.VMEM((d, n_qk), BF16),
            pltpu.VMEM((n_v, d), BF16),
            pltpu.VMEM((d, n_ug), BF16),
            pltpu.VMEM((W_SLOTS, W_CHUNK, w_in.shape[1]), F32),
            pltpu.SemaphoreType.DMA((W_SLOTS,)),
        ],
        compiler_params=_compiler_params(("arbitrary",)),
        name="in_proj",
    )(x2d, g, w_in, *cast_weights)
    return outs[0], outs[1], outs[2], outs[3:]


def _t5_bucket_np(rel):
    half = NUM_BUCKETS // 2
    max_exact = half // 2
    ret = np.where(rel > 0, half, 0)
    n = np.abs(rel)
    nf = np.maximum(n, 1).astype(np.float32)
    large = max_exact + (np.log(nf / np.float32(max_exact))
                         / np.float32(math.log(MAX_DISTANCE / max_exact))
                         * np.float32(half - max_exact)).astype(np.int32)
    large = np.minimum(large, half - 1)
    return (ret + np.where(n < max_exact, n, large)).astype(np.int32)


def _near_bucket_table(seq):
    i = np.arange(ATTN_TQ)[None, :]
    j = np.arange(2 * ATTN_TQ)[:, None] - ATTN_TQ
    visible = np.floor_divide(j, CHUNK) <= i // CHUNK
    table = np.where(visible, _t5_bucket_np(j - i), -1).astype(np.int32)
    far = np.unique(_t5_bucket_np(-np.arange(ATTN_TQ + 1, max(seq, ATTN_TQ + 2))))
    assert far.size == 1, "far keys must share one bias bucket"
    return table, int(far[0])


def _bias_tile_kernel(bucket_ref, rb_ref, out_ref, *, buckets, far_bucket):
    bucket = bucket_ref[...]
    for h in range(DIFF_HEADS):
        shift = rb_ref[far_bucket, h]
        tile = jnp.full(bucket.shape, NEG_INF, F32)
        for b in buckets:
            tile = jnp.where(bucket == b, (rb_ref[b, h] - shift) * LOG2_E, tile)
        out_ref[h] = tile


def _bias_tile(rel_bias, seq):
    table, far_bucket = _near_bucket_table(seq)
    buckets = tuple(int(b) for b in np.unique(table) if b >= 0)
    return pl.pallas_call(
        functools.partial(_bias_tile_kernel, buckets=buckets, far_bucket=far_bucket),
        in_specs=[pl.BlockSpec(memory_space=pltpu.VMEM),
                  pl.BlockSpec(memory_space=pltpu.SMEM)],
        out_specs=pl.BlockSpec(memory_space=pltpu.VMEM),
        out_shape=jax.ShapeDtypeStruct((DIFF_HEADS,) + table.shape, F32),
        name="bias_tile",
    )(jnp.asarray(table), rel_bias)


def _attn_kernel(q_ref, k_ref, vt_ref, bias_ref, lq1_ref, lk1_ref, lq2_ref, lk2_ref,
                 g_ref, *rest, n_q_tiles, n_cast, lam_init):
    w_f32_refs = rest[:n_cast]
    o_ref = rest[n_cast]
    w_bf16_refs = rest[n_cast + 1:2 * n_cast + 1]
    m_sc, l_sc, acc_sc = rest[2 * n_cast + 1:]
    tq = ATTN_TQ

    def key_tile(j, kt, bias_rows, init):
        q_rows = slice(j * tq, (j + 1) * tq)
        for m in range(2):
            cols = slice(m * DIFF_QK_DIM, (m + 1) * DIFF_QK_DIM)
            s = lax.dot_general(k_ref[kt * tq:(kt + 1) * tq, cols], q_ref[q_rows, cols], _NT_DIMS,
                                preferred_element_type=F32)
            if bias_rows is not None:
                s = s + bias_ref[bias_rows, :]
            m_cur = jnp.max(s, axis=0, keepdims=True)
            if init:
                m_next = m_cur
            else:
                m_prev = m_sc[j, m]
                m_next = jnp.maximum(m_prev, m_cur)
            p = jnp.exp2(s - m_next)
            p_sum = jnp.sum(p, axis=0, keepdims=True)
            pv = _dot(vt_ref[kt], p.astype(BF16))
            if init:
                l_sc[j, m] = p_sum
                acc_sc[j, m] = pv
            else:
                alpha = jnp.exp2(m_prev - m_next)
                l_sc[j, m] = alpha * l_sc[j, m] + p_sum
                acc_sc[j, m] = acc_sc[j, m] * alpha + pv
            m_sc[j, m] = m_next

    for src, dst in zip(w_f32_refs, w_bf16_refs):
        dst[...] = src[...].astype(BF16)
    lam = (jnp.exp(jnp.sum(lq1_ref[...] * lk1_ref[...], axis=-1, keepdims=True))
           - jnp.exp(jnp.sum(lq2_ref[...] * lk2_ref[...], axis=-1, keepdims=True))
           + lam_init)
    for j in range(n_q_tiles):
        key_tile(j, j, slice(tq, 2 * tq), init=True)
        if j > 0:
            key_tile(j, j - 1, slice(0, tq), init=False)
        for kt in range(j - 2, -1, -1):
            key_tile(j, kt, None, init=False)
        o = acc_sc[j, 0] * (1.0 / l_sc[j, 0]) - acc_sc[j, 1] * (lam / l_sc[j, 1])
        r = lax.rsqrt(jnp.mean(o * o, axis=0, keepdims=True) + EPS) * (1.0 - lam_init)
        y = o * r * g_ref[...]
        o_ref[j * tq:(j + 1) * tq, :] = y.T.astype(BF16)


def _attention(zqk, vt, bias_tile, lq1, lk1, lq2, lk2, subln_g_col, bsz, seq, lam_init,
               cast_weights):
    tq = ATTN_TQ
    assert seq % tq == 0
    nq = seq // tq
    nh = DIFF_HEADS
    vec = pl.BlockSpec((1, DIFF_QK_DIM), lambda b, h: (0, 0))
    slab_specs = _slab_specs(cast_weights, bsz * nh, lambda b, h: (b * nh + h, 0))
    outs = pl.pallas_call(
        functools.partial(_attn_kernel, n_q_tiles=nq, n_cast=len(cast_weights),
                          lam_init=lam_init),
        grid=(bsz, nh),
        in_specs=[
            pl.BlockSpec((seq, DIFF_V_DIM), lambda b, h: (b, h)),
            pl.BlockSpec((seq, DIFF_V_DIM), lambda b, h: (b, nh + h)),
            pl.BlockSpec((None, nq, DIFF_V_DIM, tq), lambda b, h: (b, 0, h, 0)),
            pl.BlockSpec((None, 2 * tq, tq), lambda b, h: (h, 0, 0)),
            vec, vec, vec, vec,
            pl.BlockSpec((DIFF_V_DIM, 1), lambda b, h: (0, 0)),
        ] + slab_specs,
        out_specs=[pl.BlockSpec((seq, DIFF_V_DIM), lambda b, h: (b, h))] + slab_specs,
        out_shape=[jax.ShapeDtypeStruct((bsz * seq, DIFF_WIDTH), BF16)]
        + [jax.ShapeDtypeStruct(w.shape, BF16) for w in cast_weights],
        scratch_shapes=[
            pltpu.VMEM((nq, 2, 1, tq), F32),
            pltpu.VMEM((nq, 2, 1, tq), F32),
            pltpu.VMEM((nq, 2, DIFF_V_DIM, tq), F32),
        ],
        compiler_params=_compiler_params(("arbitrary", "arbitrary")),
        name="diff_attention",
    )(zqk, zqk, vt, bias_tile, lq1, lk1, lq2, lk2, subln_g_col, *cast_weights)
    return outs[0], outs[1:]


def _mix_out_kernel(od_ref, gu_ref, gv_ref, x_ref, lng_ref, lnb_ref, ws_ref, bs_ref,
                    wo_ref, h1_ref, gm_sc):
    tm = od_ref.shape[0]
    h1_attn = x_ref[...] + _dot(od_ref[...], wo_ref[:DIFF_WIDTH, :])
    gv = gv_ref[...].astype(F32)
    mu = jnp.mean(gv, axis=-1, keepdims=True)
    xc = gv - mu
    vn = (xc * lax.rsqrt(jnp.mean(xc * xc, axis=-1, keepdims=True) + EPS)
          * lng_ref[...] + lnb_ref[...]).astype(BF16)

    row = lax.broadcasted_iota(jnp.int32, (GMLP_CHUNK, GMLP_CHUNK), 0)
    col = lax.broadcasted_iota(jnp.int32, (GMLP_CHUNK, GMLP_CHUNK), 1)
    causal = (col // CHUNK) <= (row // CHUNK)
    for g in range(GMLP_GROUPS):
        ws_g = jnp.where(causal, ws_ref[g], 0.0).astype(BF16)
        cols = slice(g * GMLP_GROUP_DIM, (g + 1) * GMLP_GROUP_DIM)
        for r in range(tm // GMLP_CHUNK):
            rows = slice(r * GMLP_CHUNK, (r + 1) * GMLP_CHUNK)
            s = _dot(ws_g, vn[rows, cols]) + bs_ref[g]
            gm_sc[rows, cols] = (gu_ref[rows, cols].astype(F32) * s).astype(BF16)

    h1_ref[...] = h1_attn + _dot(gm_sc[...], wo_ref[DIFF_WIDTH:, :])


def _mix_out(o_diff, zug, x2d, ln_g, ln_b, ws, bs, w_out_bf16):
    m, d = x2d.shape
    tm = MIX_TM
    assert m % tm == 0 and tm % GMLP_CHUNK == 0
    full = lambda shape: pl.BlockSpec(shape, lambda i: (0,) * len(shape))
    return pl.pallas_call(
        _mix_out_kernel,
        grid=(m // tm,),
        in_specs=[
            pl.BlockSpec((tm, DIFF_WIDTH), lambda i: (i, 0)),
            pl.BlockSpec((tm, GMLP_WIDTH), lambda i: (i, 0)),
            pl.BlockSpec((tm, GMLP_WIDTH), lambda i: (i, 1)),
            pl.BlockSpec((tm, d), lambda i: (i, 0)),
            full((1, GMLP_WIDTH)), full((1, GMLP_WIDTH)),
            full((GMLP_GROUPS, GMLP_CHUNK, GMLP_CHUNK)),
            full((GMLP_GROUPS, GMLP_CHUNK, 1)),
            full((DIFF_WIDTH + GMLP_WIDTH, d)),
        ],
        out_specs=pl.BlockSpec((tm, d), lambda i: (i, 0)),
        out_shape=jax.ShapeDtypeStruct((m, d), F32),
        scratch_shapes=[pltpu.VMEM((tm, GMLP_WIDTH), BF16)],
        compiler_params=_compiler_params(("arbitrary",)),
        name="mix_out",
    )(o_diff, zug, zug, x2d, ln_g, ln_b, ws, bs, w_out_bf16)


def _ffn_kernel(h1_ref, fg_ref, w1_ref, w3_ref, w2_ref, o_ref, f_sc):
    k = pl.program_id(1)

    def hidden_tile(f, base):
        act = (jax.nn.silu(_dot(f, w1_ref[...])) * _dot(f, w3_ref[...])).astype(BF16)
        o_ref[...] = base + _dot(act, w2_ref[...])

    @pl.when(k == 0)
    def _():
        h1 = h1_ref[...]
        f = (_rms_scale(h1) * fg_ref[...]).astype(BF16)
        f_sc[...] = f
        hidden_tile(f, h1)

    @pl.when(k > 0)
    def _():
        hidden_tile(f_sc[...], o_ref[...])


def _ffn(h1, ffn_g, w1, w3, w2):
    m, d = h1.shape
    hid = w1.shape[1]
    tm, th = FFN_TM, FFN_TH
    assert m % tm == 0 and hid % th == 0
    return pl.pallas_call(
        _ffn_kernel,
        grid=(m // tm, hid // th),
        in_specs=[
            pl.BlockSpec((tm, d), lambda i, k: (i, 0)),
            pl.BlockSpec((1, d), lambda i, k: (0, 0)),
            pl.BlockSpec((d, th), lambda i, k: (0, k)),
            pl.BlockSpec((d, th), lambda i, k: (0, k)),
            pl.BlockSpec((th, d), lambda i, k: (k, 0)),
        ],
        out_specs=pl.BlockSpec((tm, d), lambda i, k: (i, 0)),
        out_shape=jax.ShapeDtypeStruct((m, d), F32),
        scratch_shapes=[pltpu.VMEM((tm, d), BF16)],
        compiler_params=_compiler_params(("arbitrary", "arbitrary")),
        name="ffn",
    )(h1, ffn_g, w1, w3, w2)


def _ple_kernel(h2_ref, p_ref, wg_ref, wu_ref, g_ref, o_ref, *, final_norm):
    h2 = h2_ref[...]
    gate = _dot(h2.astype(BF16), wg_ref[...])
    up = _dot(p_ref[...].astype(BF16), wu_ref[...])
    h3 = h2 + up * jax.nn.sigmoid(gate)
    if final_norm:
        h3 = _rms_scale(h3) * g_ref[...]
    o_ref[...] = h3


def _ple(h2, p2d, w_gate, w_up, final_g, final_norm):
    m, d = h2.shape
    pd = p2d.shape[1]
    tm = PLE_TM
    assert m % tm == 0
    full = lambda shape: pl.BlockSpec(shape, lambda i: (0,) * len(shape))
    return pl.pallas_call(
        functools.partial(_ple_kernel, final_norm=final_norm),
        grid=(m // tm,),
        in_specs=[
            pl.BlockSpec((tm, d), lambda i: (i, 0)),
            pl.BlockSpec((tm, pd), lambda i: (i, 0)),
            full((d, d)), full((pd, d)), full((1, d)),
        ],
        out_specs=pl.BlockSpec((tm, d), lambda i: (i, 0)),
        out_shape=jax.ShapeDtypeStruct((m, d), F32),
        compiler_params=_compiler_params(("arbitrary",)),
        name="ple",
    )(h2, p2d, w_gate, w_up, final_g)


def kernel(x, p, w_in, w_out, attn_norm_g, ffn_norm_g, final_norm_g, lambda_q1, lambda_k1,
           lambda_q2, lambda_k2, subln_g, rel_bias, gmlp_ln_g, gmlp_ln_b, gmlp_ws, gmlp_b,
           ffn_w1, ffn_w3, ffn_w2, pl_w_up, pl_w_gate):
    bsz, seq, d = x.shape
    depth = w_in.shape[0]
    qk_cols = DIFF_HEADS * 2 * DIFF_QK_DIM
    assert w_in.shape[2] == 2 * qk_cols + DIFF_WIDTH + 2 * GMLP_WIDTH
    assert seq % GMLP_CHUNK == 0
    row = lambda v: v.reshape(1, -1)

    bias_tile = _bias_tile(rel_bias, seq)
    h = x.reshape(bsz * seq, d)
    for i in range(depth):
        lam_init = 0.8 - 0.6 * math.exp(-0.3 * i)
        zqk, vt, zug, (w2, wo, wg) = _in_proj(h, row(attn_norm_g[i]), w_in[i], bsz, seq,
                                              cast_weights=(ffn_w2[i], w_out[i], pl_w_gate[i]))
        o_diff, (w1, w3) = _attention(
            zqk, vt, bias_tile, row(lambda_q1[i]), row(lambda_k1[i]), row(lambda_q2[i]),
            row(lambda_k2[i]), subln_g[i].reshape(-1, 1), bsz, seq, lam_init,
            cast_weights=(ffn_w1[i], ffn_w3[i]))
        h1 = _mix_out(o_diff, zug, h, row(gmlp_ln_g[i]), row(gmlp_ln_b[i]), gmlp_ws[i],
                      gmlp_b[i].reshape(GMLP_GROUPS, GMLP_CHUNK, 1), wo)
        h2 = _ffn(h1, row(ffn_norm_g[i]), w1, w3, w2)
        h = _ple(h2, p[i].reshape(bsz * seq, -1), wg, pl_w_up[i].astype(BF16),
                 row(final_norm_g), final_norm=(i == depth - 1))
    return h.reshape(bsz, seq, d)
```

```python
import functools
import math

import numpy as np
import jax
import jax.numpy as jnp
from jax import lax
from jax.experimental import pallas as pl
from jax.experimental.pallas import tpu as pltpu

F32 = jnp.float32
BF16 = jnp.bfloat16

CHUNK = 64
DIFF_HEADS = 4
DIFF_QK_DIM = 128
DIFF_V_DIM = 2 * DIFF_QK_DIM
DIFF_WIDTH = DIFF_HEADS * DIFF_V_DIM
GMLP_CHUNK = 128
GMLP_GROUPS = 8
GMLP_GROUP_DIM = 128
GMLP_WIDTH = GMLP_GROUPS * GMLP_GROUP_DIM
NUM_BUCKETS = 32
MAX_DISTANCE = 128
EPS = 1e-6
NEG_INF = -1e30
ATTN_SCALE = DIFF_QK_DIM ** -0.5
LOG2_E = math.log2(math.e)
Q_SCALE = ATTN_SCALE * LOG2_E

VMEM_LIMIT_BYTES = 56 * 1024 * 1024
BF16_SUBLANES = 16

IN_TM, IN_TN = 512, 1024
W_CHUNK, W_SLOTS = 128, 3
ATTN_TQ = 256
MIX_TM = 512
FFN_TM, FFN_TH = 1024, 512
PLE_TM = 512

_NT_DIMS = (((1,), (1,)), ((), ()))


def _dot(a, b):
    return jnp.dot(a, b, preferred_element_type=F32)


def _rms_scale(x):
    return x * lax.rsqrt(jnp.mean(x * x, axis=-1, keepdims=True) + EPS)


def _compiler_params(semantics):
    return pltpu.CompilerParams(dimension_semantics=semantics,
                                vmem_limit_bytes=VMEM_LIMIT_BYTES)


def _in_proj_kernel(x_ref, g_ref, w_hbm, *rest, n_cast):
    w_f32_refs = rest[:n_cast]
    qk_ref, vt_ref, ug_ref = rest[n_cast:n_cast + 3]
    w_bf16_refs = rest[n_cast + 3:2 * n_cast + 3]
    wqk_sc, wvt_sc, wug_sc, stage_sc, sem = rest[2 * n_cast + 3:]
    n_qk, n_v = wqk_sc.shape[1], wvt_sc.shape[0]

    n_slots = stage_sc.shape[0]

    def chunk_copy(c):
        slot = c % n_slots
        return pltpu.make_async_copy(w_hbm.at[c * W_CHUNK:(c + 1) * W_CHUNK, :],
                                     stage_sc.at[slot], sem.at[slot])

    @pl.when(pl.program_id(0) == 0)
    def _():
        n_chunks = w_hbm.shape[0] // W_CHUNK
        for c in range(n_slots - 1):
            chunk_copy(c).start()
        for c in range(n_chunks):
            if c + n_slots - 1 < n_chunks:
                chunk_copy(c + n_slots - 1).start()
            chunk_copy(c).wait()
            rows = slice(c * W_CHUNK, (c + 1) * W_CHUNK)
            blk = stage_sc[c % n_slots]
            wqk_sc[rows, :] = blk[:, :n_qk].astype(BF16)
            wvt_sc[:, rows] = blk[:, n_qk:n_qk + n_v].T.astype(BF16)
            wug_sc[rows, :] = blk[:, n_qk + n_v:].astype(BF16)

    for src, dst in zip(w_f32_refs, w_bf16_refs):
        dst[...] = src[...].astype(BF16)
    a = (_rms_scale(x_ref[...]) * g_ref[...]).astype(BF16)
    for c in range(0, n_qk, IN_TN):
        zc = _dot(a, wqk_sc[:, c:c + IN_TN])
        if c < DIFF_WIDTH:
            zc = zc * Q_SCALE
        qk_ref[:, c:c + IN_TN] = zc.astype(BF16)
    vt = lax.dot_general(wvt_sc[...], a, _NT_DIMS, preferred_element_type=F32).astype(BF16)
    for t in range(vt_ref.shape[0]):
        vt_ref[t] = vt[:, t * ATTN_TQ:(t + 1) * ATTN_TQ]
    for c in range(0, wug_sc.shape[1], IN_TN):
        ug_ref[:, c:c + IN_TN] = jax.nn.gelu(_dot(a, wug_sc[:, c:c + IN_TN])).astype(BF16)


def _slab_specs(weights, n_steps, index_map):
    specs = []
    for w in weights:
        rows, cols = w.shape
        assert rows % (n_steps * BF16_SUBLANES) == 0
        specs.append(pl.BlockSpec((rows // n_steps, cols), index_map))
    return specs


def _in_proj(x2d, g, w_in, bsz, seq, cast_weights):
    m, d = x2d.shape
    tm = IN_TM
    n_qk, n_v, n_ug = 2 * DIFF_WIDTH, DIFF_WIDTH, 2 * GMLP_WIDTH
    assert w_in.shape == (d, n_qk + n_v + n_ug) and d % W_CHUNK == 0 and d // W_CHUNK >= W_SLOTS
    assert seq % tm == 0 and tm % ATTN_TQ == 0
    assert n_qk % IN_TN == 0 and n_ug % IN_TN == 0 and DIFF_WIDTH % IN_TN == 0
    tiles_per_seq = seq // tm
    kt = tm // ATTN_TQ
    slab_specs = _slab_specs(cast_weights, m // tm, lambda i: (i, 0))
    outs = pl.pallas_call(
        functools.partial(_in_proj_kernel, n_cast=len(cast_weights)),
        grid=(m // tm,),
        in_specs=[
            pl.BlockSpec((tm, d), lambda i: (i, 0)),
            pl.BlockSpec((1, d), lambda i: (0, 0)),
            pl.BlockSpec(memory_space=pl.ANY),
        ] + slab_specs,
        out_specs=[
            pl.BlockSpec((tm, n_qk), lambda i: (i, 0)),
            pl.BlockSpec((None, kt, DIFF_WIDTH, ATTN_TQ),
                         lambda i: (i // tiles_per_seq, i % tiles_per_seq, 0, 0)),
            pl.BlockSpec((tm, n_ug), lambda i: (i, 0)),
        ] + slab_specs,
        out_shape=[
            jax.ShapeDtypeStruct((m, n_qk), BF16),
            jax.ShapeDtypeStruct((bsz, seq // ATTN_TQ, DIFF_WIDTH, ATTN_TQ), BF16),
            jax.ShapeDtypeStruct((m, n_ug), BF16),
        ] + [jax.ShapeDtypeStruct(w.shape, BF16) for w in cast_weights],
        scratch_shapes=[
            pltpu.VMEM((d, n_qk), BF16),
            pltpu.VMEM((n_v, d), BF16),
            pltpu.VMEM((d, n_ug), BF16),
            pltpu.VMEM((W_SLOTS, W_CHUNK, w_in.shape[1]), F32),
            pltpu.SemaphoreType.DMA((W_SLOTS,)),
        ],
        compiler_params=_compiler_params(("arbitrary",)),
        name="in_proj",
    )(x2d, g, w_in, *cast_weights)
    return outs[0], outs[1], outs[2], outs[3:]


def _t5_bucket_np(rel):
    half = NUM_BUCKETS // 2
    max_exact = half // 2
    ret = np.where(rel > 0, half, 0)
    n = np.abs(rel)
    nf = np.maximum(n, 1).astype(np.float32)
    large = max_exact + (np.log(nf / np.float32(max_exact))
                         / np.float32(math.log(MAX_DISTANCE / max_exact))
                         * np.float32(half - max_exact)).astype(np.int32)
    large = np.minimum(large, half - 1)
    return (ret + np.where(n < max_exact, n, large)).astype(np.int32)


def _near_bucket_table(seq):
    i = np.arange(ATTN_TQ)[None, :]
    j = np.arange(2 * ATTN_TQ)[:, None] - ATTN_TQ
    visible = np.floor_divide(j, CHUNK) <= i // CHUNK
    table = np.where(visible, _t5_bucket_np(j - i), -1).astype(np.int32)
    far = np.unique(_t5_bucket_np(-np.arange(ATTN_TQ + 1, max(seq, ATTN_TQ + 2))))
    assert far.size == 1, "far keys must share one bias bucket"
    return table, int(far[0])


def _bias_tile_kernel(bucket_ref, rb_ref, out_ref, *, buckets, far_bucket):
    bucket = bucket_ref[...]
    for h in range(DIFF_HEADS):
        shift = rb_ref[far_bucket, h]
        tile = jnp.full(bucket.shape, NEG_INF, F32)
        for b in buckets:
            tile = jnp.where(bucket == b, (rb_ref[b, h] - shift) * LOG2_E, tile)
        out_ref[h] = tile


def _bias_tile(rel_bias, seq):
    table, far_bucket = _near_bucket_table(seq)
    buckets = tuple(int(b) for b in np.unique(table) if b >= 0)
    return pl.pallas_call(
        functools.partial(_bias_tile_kernel, buckets=buckets, far_bucket=far_bucket),
        in_specs=[pl.BlockSpec(memory_space=pltpu.VMEM),
                  pl.BlockSpec(memory_space=pltpu.SMEM)],
        out_specs=pl.BlockSpec(memory_space=pltpu.VMEM),
        out_shape=jax.ShapeDtypeStruct((DIFF_HEADS,) + table.shape, F32),
        name="bias_tile",
    )(jnp.asarray(table), rel_bias)


def _attn_kernel(q_ref, k_ref, vt_ref, bias_ref, lq1_ref, lk1_ref, lq2_ref, lk2_ref,
                 g_ref, *rest, n_q_tiles, n_cast, lam_init):
    w_f32_refs = rest[:n_cast]
    o_ref = rest[n_cast]
    w_bf16_refs = rest[n_cast + 1:2 * n_cast + 1]
    m_sc, l_sc, acc_sc = rest[2 * n_cast + 1:]
    tq = ATTN_TQ

    def key_tile(j, kt, bias_rows, init):
        q_rows = slice(j * tq, (j + 1) * tq)
        for m in range(2):
            cols = slice(m * DIFF_QK_DIM, (m + 1) * DIFF_QK_DIM)
            s = lax.dot_general(k_ref[kt * tq:(kt + 1) * tq, cols], q_ref[q_rows, cols], _NT_DIMS,
                                preferred_element_type=F32)
            if bias_rows is not None:
                s = s + bias_ref[bias_rows, :]
            m_cur = jnp.max(s, axis=0, keepdims=True)
            if init:
                m_next = m_cur
            else:
                m_prev = m_sc[j, m]
                m_next = jnp.maximum(m_prev, m_cur)
            p = jnp.exp2(s - m_next)
            p_sum = jnp.sum(p, axis=0, keepdims=True)
            pv = _dot(vt_ref[kt], p.astype(BF16))
            if init:
                l_sc[j, m] = p_sum
                acc_sc[j, m] = pv
            else:
                alpha = jnp.exp2(m_prev - m_next)
                l_sc[j, m] = alpha * l_sc[j, m] + p_sum
                acc_sc[j, m] = acc_sc[j, m] * alpha + pv
            m_sc[j, m] = m_next

    for src, dst in zip(w_f32_refs, w_bf16_refs):
        dst[...] = src[...].astype(BF16)
    lam = (jnp.exp(jnp.sum(lq1_ref[...] * lk1_ref[...], axis=-1, keepdims=True))
           - jnp.exp(jnp.sum(lq2_ref[...] * lk2_ref[...], axis=-1, keepdims=True))
           + lam_init)
    for j in range(n_q_tiles):
        key_tile(j, j, slice(tq, 2 * tq), init=True)
        if j > 0:
            key_tile(j, j - 1, slice(0, tq), init=False)
        for kt in range(j - 2, -1, -1):
            key_tile(j, kt, None, init=False)
        o = acc_sc[j, 0] * (1.0 / l_sc[j, 0]) - acc_sc[j, 1] * (lam / l_sc[j, 1])
        r = lax.rsqrt(jnp.mean(o * o, axis=0, keepdims=True) + EPS) * (1.0 - lam_init)
        y = o * r * g_ref[...]
        o_ref[j * tq:(j + 1) * tq, :] = y.T.astype(BF16)


def _attention(zqk, vt, bias_tile, lq1, lk1, lq2, lk2, subln_g_col, bsz, seq, lam_init,
               cast_weights):
    tq = ATTN_TQ
    assert seq % tq == 0
    nq = seq // tq
    nh = DIFF_HEADS
    vec = pl.BlockSpec((1, DIFF_QK_DIM), lambda b, h: (0, 0))
    slab_specs = _slab_specs(cast_weights, bsz * nh, lambda b, h: (b * nh + h, 0))
    outs = pl.pallas_call(
        functools.partial(_attn_kernel, n_q_tiles=nq, n_cast=len(cast_weights),
                          lam_init=lam_init),
        grid=(bsz, nh),
        in_specs=[
            pl.BlockSpec((seq, DIFF_V_DIM), lambda b, h: (b, h)),
            pl.BlockSpec((seq, DIFF_V_DIM), lambda b, h: (b, nh + h)),
            pl.BlockSpec((None, nq, DIFF_V_DIM, tq), lambda b, h: (b, 0, h, 0)),
            pl.BlockSpec((None, 2 * tq, tq), lambda b, h: (h, 0, 0)),
            vec, vec, vec, vec,
            pl.BlockSpec((DIFF_V_DIM, 1), lambda b, h: (0, 0)),
        ] + slab_specs,
        out_specs=[pl.BlockSpec((seq, DIFF_V_DIM), lambda b, h: (b, h))] + slab_specs,
        out_shape=[jax.ShapeDtypeStruct((bsz * seq, DIFF_WIDTH), BF16)]
        + [jax.ShapeDtypeStruct(w.shape, BF16) for w in cast_weights],
        scratch_shapes=[
            pltpu.VMEM((nq, 2, 1, tq), F32),
            pltpu.VMEM((nq, 2, 1, tq), F32),
            pltpu.VMEM((nq, 2, DIFF_V_DIM, tq), F32),
        ],
        compiler_params=_compiler_params(("arbitrary", "arbitrary")),
        name="diff_attention",
    )(zqk, zqk, vt, bias_tile, lq1, lk1, lq2, lk2, subln_g_col, *cast_weights)
    return outs[0], outs[1:]


def _mix_out_kernel(od_ref, gu_ref, gv_ref, x_ref, lng_ref, lnb_ref, ws_ref, bs_ref,
                    wo_ref, h1_ref, gm_sc):
    tm = od_ref.shape[0]
    h1_attn = x_ref[...] + _dot(od_ref[...], wo_ref[:DIFF_WIDTH, :])
    gv = gv_ref[...].astype(F32)
    mu = jnp.mean(gv, axis=-1, keepdims=True)
    xc = gv - mu
    vn = (xc * lax.rsqrt(jnp.mean(xc * xc, axis=-1, keepdims=True) + EPS)
          * lng_ref[...] + lnb_ref[...]).astype(BF16)

    row = lax.broadcasted_iota(jnp.int32, (GMLP_CHUNK, GMLP_CHUNK), 0)
    col = lax.broadcasted_iota(jnp.int32, (GMLP_CHUNK, GMLP_CHUNK), 1)
    causal = (col // CHUNK) <= (row // CHUNK)
    for g in range(GMLP_GROUPS):
        ws_g = jnp.where(causal, ws_ref[g], 0.0).astype(BF16)
        cols = slice(g * GMLP_GROUP_DIM, (g + 1) * GMLP_GROUP_DIM)
        for r in range(tm // GMLP_CHUNK):
            rows = slice(r * GMLP_CHUNK, (r + 1) * GMLP_CHUNK)
            s = _dot(ws_g, vn[rows, cols]) + bs_ref[g]
            gm_sc[rows, cols] = (gu_ref[rows, cols].astype(F32) * s).astype(BF16)

    h1_ref[...] = h1_attn + _dot(gm_sc[...], wo_ref[DIFF_WIDTH:, :])


def _mix_out(o_diff, zug, x2d, ln_g, ln_b, ws, bs, w_out_bf16):
    m, d = x2d.shape
    tm = MIX_TM
    assert m % tm == 0 and tm % GMLP_CHUNK == 0
    full = lambda shape: pl.BlockSpec(shape, lambda i: (0,) * len(shape))
    return pl.pallas_call(
        _mix_out_kernel,
        grid=(m // tm,),
        in_specs=[
            pl.BlockSpec((tm, DIFF_WIDTH), lambda i: (i, 0)),
            pl.BlockSpec((tm, GMLP_WIDTH), lambda i: (i, 0)),
            pl.BlockSpec((tm, GMLP_WIDTH), lambda i: (i, 1)),
            pl.BlockSpec((tm, d), lambda i: (i, 0)),
            full((1, GMLP_WIDTH)), full((1, GMLP_WIDTH)),
            full((GMLP_GROUPS, GMLP_CHUNK, GMLP_CHUNK)),
            full((GMLP_GROUPS, GMLP_CHUNK, 1)),
            full((DIFF_WIDTH + GMLP_WIDTH, d)),
        ],
        out_specs=pl.BlockSpec((tm, d), lambda i: (i, 0)),
        out_shape=jax.ShapeDtypeStruct((m, d), F32),
        scratch_shapes=[pltpu.VMEM((tm, GMLP_WIDTH), BF16)],
        compiler_params=_compiler_params(("arbitrary",)),
        name="mix_out",
    )(o_diff, zug, zug, x2d, ln_g, ln_b, ws, bs, w_out_bf16)


def _ffn_kernel(h1_ref, fg_ref, w1_ref, w3_ref, w2_ref, o_ref, f_sc):
    k = pl.program_id(1)

    def hidden_tile(f, base):
        act = (jax.nn.silu(_dot(f, w1_ref[...])) * _dot(f, w3_ref[...])).astype(BF16)
        o_ref[...] = base + _dot(act, w2_ref[...])

    @pl.when(k == 0)
    def _():
        h1 = h1_ref[...]
        f = (_rms_scale(h1) * fg_ref[...]).astype(BF16)
        f_sc[...] = f
        hidden_tile(f, h1)

    @pl.when(k > 0)
    def _():
        hidden_tile(f_sc[...], o_ref[...])


def _ffn(h1, ffn_g, w1, w3, w2):
    m, d = h1.shape
    hid = w1.shape[1]
    tm, th = FFN_TM, FFN_TH
    assert m % tm == 0 and hid % th == 0
    return pl.pallas_call(
        _ffn_kernel,
        grid=(m // tm, hid // th),
        in_specs=[
            pl.BlockSpec((tm, d), lambda i, k: (i, 0)),
            pl.BlockSpec((1, d), lambda i, k: (0, 0)),
            pl.BlockSpec((d, th), lambda i, k: (0, k)),
            pl.BlockSpec((d, th), lambda i, k: (0, k)),
            pl.BlockSpec((th, d), lambda i, k: (k, 0)),
        ],
        out_specs=pl.BlockSpec((tm, d), lambda i, k: (i, 0)),
        out_shape=jax.ShapeDtypeStruct((m, d), F32),
        scratch_shapes=[pltpu.VMEM((tm, d), BF16)],
        compiler_params=_compiler_params(("arbitrary", "arbitrary")),
        name="ffn",
    )(h1, ffn_g, w1, w3, w2)


def _ple_kernel(h2_ref, p_ref, wg_ref, wu_ref, g_ref, o_ref, *, final_norm):
    h2 = h2_ref[...]
    gate = _dot(h2.astype(BF16), wg_ref[...])
    up = _dot(p_ref[...].astype(BF16), wu_ref[...])
    h3 = h2 + up * jax.nn.sigmoid(gate)
    if final_norm:
        h3 = _rms_scale(h3) * g_ref[...]
    o_ref[...] = h3


def _ple(h2, p2d, w_gate, w_up, final_g, final_norm):
    m, d = h2.shape
    pd = p2d.shape[1]
    tm = PLE_TM
    assert m % tm == 0
    full = lambda shape: pl.BlockSpec(shape, lambda i: (0,) * len(shape))
    return pl.pallas_call(
        functools.partial(_ple_kernel, final_norm=final_norm),
        grid=(m // tm,),
        in_specs=[
            pl.BlockSpec((tm, d), lambda i: (i, 0)),
            pl.BlockSpec((tm, pd), lambda i: (i, 0)),
            full((d, d)), full((pd, d)), full((1, d)),
        ],
        out_specs=pl.BlockSpec((tm, d), lambda i: (i, 0)),
        out_shape=jax.ShapeDtypeStruct((m, d), F32),
        compiler_params=_compiler_params(("arbitrary",)),
        name="ple",
    )(h2, p2d, w_gate, w_up, final_g)


def kernel(x, p, w_in, w_out, attn_norm_g, ffn_norm_g, final_norm_g, lambda_q1, lambda_k1,
           lambda_q2, lambda_k2, subln_g, rel_bias, gmlp_ln_g, gmlp_ln_b, gmlp_ws, gmlp_b,
           ffn_w1, ffn_w3, ffn_w2, pl_w_up, pl_w_gate):
    bsz, seq, d = x.shape
    depth = w_in.shape[0]
    qk_cols = DIFF_HEADS * 2 * DIFF_QK_DIM
    assert w_in.shape[2] == 2 * qk_cols + DIFF_WIDTH + 2 * GMLP_WIDTH
    assert seq % GMLP_CHUNK == 0
    row = lambda v: v.reshape(1, -1)

    bias_tile = _bias_tile(rel_bias, seq)
    h = x.reshape(bsz * seq, d)
    for i in range(depth):
        lam_init = 0.8 - 0.6 * math.exp(-0.3 * i)
        zqk, vt, zug, (w2,) = _in_proj(h, row(attn_norm_g[i]), w_in[i], bsz, seq,
                                       cast_weights=(ffn_w2[i],))
        o_diff, (w1, w3, wo, wg) = _attention(
            zqk, vt, bias_tile, row(lambda_q1[i]), row(lambda_k1[i]), row(lambda_q2[i]),
            row(lambda_k2[i]), subln_g[i].reshape(-1, 1), bsz, seq, lam_init,
            cast_weights=(ffn_w1[i], ffn_w3[i], w_out[i], pl_w_gate[i]))
        h1 = _mix_out(o_diff, zug, h, row(gmlp_ln_g[i]), row(gmlp_ln_b[i]), gmlp_ws[i],
                      gmlp_b[i].reshape(GMLP_GROUPS, GMLP_CHUNK, 1), wo)
        h2 = _ffn(h1, row(ffn_norm_g[i]), w1, w3, w2)
        h = _ple(h2, p[i].reshape(bsz * seq, -1), wg, pl_w_up[i].astype(BF16),
                 row(final_norm_g), final_norm=(i == depth - 1))
    return h.reshape(bsz, seq, d)
```

```python
import functools
import math

import numpy as np
import jax
import jax.numpy as jnp
from jax import lax
from jax.experimental import pallas as pl
from jax.experimental.pallas import tpu as pltpu

F32 = jnp.float32
BF16 = jnp.bfloat16

CHUNK = 64
DIFF_HEADS = 4
DIFF_QK_DIM = 128
DIFF_V_DIM = 2 * DIFF_QK_DIM
DIFF_WIDTH = DIFF_HEADS * DIFF_V_DIM
GMLP_CHUNK = 128
GMLP_GROUPS = 8
GMLP_GROUP_DIM = 128
GMLP_WIDTH = GMLP_GROUPS * GMLP_GROUP_DIM
NUM_BUCKETS = 32
MAX_DISTANCE = 128
EPS = 1e-6
NEG_INF = -1e30
ATTN_SCALE = DIFF_QK_DIM ** -0.5
LOG2_E = math.log2(math.e)
Q_SCALE = ATTN_SCALE * LOG2_E

VMEM_LIMIT_BYTES = 56 * 1024 * 1024
BF16_SUBLANES = 16

IN_TM, IN_TN = 512, 1024
W_CHUNK, W_SLOTS = 128, 3
ATTN_TQ = 256
ATTN_HEADS_PER_STEP = 2
MIX_TM = 512
FFN_TM, FFN_TH = 1024, 512
PLE_TM = 512

_NT_DIMS = (((1,), (1,)), ((), ()))


def _dot(a, b):
    return jnp.dot(a, b, preferred_element_type=F32)


def _rms_scale(x):
    return x * lax.rsqrt(jnp.mean(x * x, axis=-1, keepdims=True) + EPS)


def _compiler_params(semantics):
    return pltpu.CompilerParams(dimension_semantics=semantics,
                                vmem_limit_bytes=VMEM_LIMIT_BYTES)


def _in_proj_kernel(x_ref, g_ref, w_hbm, *rest, n_cast):
    w_f32_refs = rest[:n_cast]
    qk_ref, vt_ref, ug_ref = rest[n_cast:n_cast + 3]
    w_bf16_refs = rest[n_cast + 3:2 * n_cast + 3]
    wqk_sc, wvt_sc, wug_sc, stage_sc, sem = rest[2 * n_cast + 3:]
    n_qk, n_v = wqk_sc.shape[1], wvt_sc.shape[0]

    n_slots = stage_sc.shape[0]

    def chunk_copy(c):
        slot = c % n_slots
        return pltpu.make_async_copy(w_hbm.at[c * W_CHUNK:(c + 1) * W_CHUNK, :],
                                     stage_sc.at[slot], sem.at[slot])

    @pl.when(pl.program_id(0) == 0)
    def _():
        n_chunks = w_hbm.shape[0] // W_CHUNK
        for c in range(n_slots - 1):
            chunk_copy(c).start()
        for c in range(n_chunks):
            if c + n_slots - 1 < n_chunks:
                chunk_copy(c + n_slots - 1).start()
            chunk_copy(c).wait()
            rows = slice(c * W_CHUNK, (c + 1) * W_CHUNK)
            blk = stage_sc[c % n_slots]
            wqk_sc[rows, :] = blk[:, :n_qk].astype(BF16)
            wvt_sc[:, rows] = blk[:, n_qk:n_qk + n_v].T.astype(BF16)
            wug_sc[rows, :] = blk[:, n_qk + n_v:].astype(BF16)

    for src, dst in zip(w_f32_refs, w_bf16_refs):
        dst[...] = src[...].astype(BF16)
    a = (_rms_scale(x_ref[...]) * g_ref[...]).astype(BF16)
    for c in range(0, n_qk, IN_TN):
        zc = _dot(a, wqk_sc[:, c:c + IN_TN])
        if c < DIFF_WIDTH:
            zc = zc * Q_SCALE
        qk_ref[:, c:c + IN_TN] = zc.astype(BF16)
    vt = lax.dot_general(wvt_sc[...], a, _NT_DIMS, preferred_element_type=F32).astype(BF16)
    for t in range(vt_ref.shape[0]):
        vt_ref[t] = vt[:, t * ATTN_TQ:(t + 1) * ATTN_TQ]
    for c in range(0, wug_sc.shape[1], IN_TN):
        ug_ref[:, c:c + IN_TN] = jax.nn.gelu(_dot(a, wug_sc[:, c:c + IN_TN])).astype(BF16)


def _slab_specs(weights, n_steps, index_map):
    specs = []
    for w in weights:
        rows, cols = w.shape
        assert rows % (n_steps * BF16_SUBLANES) == 0
        specs.append(pl.BlockSpec((rows // n_steps, cols), index_map))
    return specs


def _in_proj(x2d, g, w_in, bsz, seq, cast_weights):
    m, d = x2d.shape
    tm = IN_TM
    n_qk, n_v, n_ug = 2 * DIFF_WIDTH, DIFF_WIDTH, 2 * GMLP_WIDTH
    assert w_in.shape == (d, n_qk + n_v + n_ug) and d % W_CHUNK == 0 and d // W_CHUNK >= W_SLOTS
    assert seq % tm == 0 and tm % ATTN_TQ == 0
    assert n_qk % IN_TN == 0 and n_ug % IN_TN == 0 and DIFF_WIDTH % IN_TN == 0
    tiles_per_seq = seq // tm
    kt = tm // ATTN_TQ
    slab_specs = _slab_specs(cast_weights, m // tm, lambda i: (i, 0))
    outs = pl.pallas_call(
        functools.partial(_in_proj_kernel, n_cast=len(cast_weights)),
        grid=(m // tm,),
        in_specs=[
            pl.BlockSpec((tm, d), lambda i: (i, 0)),
            pl.BlockSpec((1, d), lambda i: (0, 0)),
            pl.BlockSpec(memory_space=pl.ANY),
        ] + slab_specs,
        out_specs=[
            pl.BlockSpec((tm, n_qk), lambda i: (i, 0)),
            pl.BlockSpec((None, kt, DIFF_WIDTH, ATTN_TQ),
                         lambda i: (i // tiles_per_seq, i % tiles_per_seq, 0, 0)),
            pl.BlockSpec((tm, n_ug), lambda i: (i, 0)),
        ] + slab_specs,
        out_shape=[
            jax.ShapeDtypeStruct((m, n_qk), BF16),
            jax.ShapeDtypeStruct((bsz, seq // ATTN_TQ, DIFF_WIDTH, ATTN_TQ), BF16),
            jax.ShapeDtypeStruct((m, n_ug), BF16),
        ] + [jax.ShapeDtypeStruct(w.shape, BF16) for w in cast_weights],
        scratch_shapes=[
            pltpu.VMEM((d, n_qk), BF16),
            pltpu.VMEM((n_v, d), BF16),
            pltpu.VMEM((d, n_ug), BF16),
            pltpu.VMEM((W_SLOTS, W_CHUNK, w_in.shape[1]), F32),
            pltpu.SemaphoreType.DMA((W_SLOTS,)),
        ],
        compiler_params=_compiler_params(("arbitrary",)),
        name="in_proj",
    )(x2d, g, w_in, *cast_weights)
    return outs[0], outs[1], outs[2], outs[3:]


def _t5_bucket_np(rel):
    half = NUM_BUCKETS // 2
    max_exact = half // 2
    ret = np.where(rel > 0, half, 0)
    n = np.abs(rel)
    nf = np.maximum(n, 1).astype(np.float32)
    large = max_exact + (np.log(nf / np.float32(max_exact))
                         / np.float32(math.log(MAX_DISTANCE / max_exact))
                         * np.float32(half - max_exact)).astype(np.int32)
    large = np.minimum(large, half - 1)
    return (ret + np.where(n < max_exact, n, large)).astype(np.int32)


def _near_bucket_table(seq):
    i = np.arange(ATTN_TQ)[None, :]
    j = np.arange(2 * ATTN_TQ)[:, None] - ATTN_TQ
    visible = np.floor_divide(j, CHUNK) <= i // CHUNK
    table = np.where(visible, _t5_bucket_np(j - i), -1).astype(np.int32)
    far = np.unique(_t5_bucket_np(-np.arange(ATTN_TQ + 1, max(seq, ATTN_TQ + 2))))
    assert far.size == 1, "far keys must share one bias bucket"
    return table, int(far[0])


def _bias_tile_kernel(bucket_ref, rb_ref, out_ref, *, buckets, far_bucket):
    bucket = bucket_ref[...]
    for h in range(DIFF_HEADS):
        shift = rb_ref[far_bucket, h]
        tile = jnp.full(bucket.shape, NEG_INF, F32)
        for b in buckets:
            tile = jnp.where(bucket == b, (rb_ref[b, h] - shift) * LOG2_E, tile)
        out_ref[h] = tile


def _bias_tile(rel_bias, seq):
    table, far_bucket = _near_bucket_table(seq)
    buckets = tuple(int(b) for b in np.unique(table) if b >= 0)
    return pl.pallas_call(
        functools.partial(_bias_tile_kernel, buckets=buckets, far_bucket=far_bucket),
        in_specs=[pl.BlockSpec(memory_space=pltpu.VMEM),
                  pl.BlockSpec(memory_space=pltpu.SMEM)],
        out_specs=pl.BlockSpec(memory_space=pltpu.VMEM),
        out_shape=jax.ShapeDtypeStruct((DIFF_HEADS,) + table.shape, F32),
        name="bias_tile",
    )(jnp.asarray(table), rel_bias)


def _attn_kernel(q_ref, k_ref, vt_ref, bias_ref, lq1_ref, lk1_ref, lq2_ref, lk2_ref,
                 g_ref, *rest, n_q_tiles, n_cast, lam_init):
    w_f32_refs = rest[:n_cast]
    o_ref = rest[n_cast]
    w_bf16_refs = rest[n_cast + 1:2 * n_cast + 1]
    m_sc, l_sc, acc_sc = rest[2 * n_cast + 1:]
    tq = ATTN_TQ

    def key_tile(hh, j, kt, bias_rows, init):
        q_rows = slice(j * tq, (j + 1) * tq)
        for m in range(2):
            c = 2 * hh + m
            cols = slice(c * DIFF_QK_DIM, (c + 1) * DIFF_QK_DIM)
            s = lax.dot_general(k_ref[kt * tq:(kt + 1) * tq, cols], q_ref[q_rows, cols], _NT_DIMS,
                                preferred_element_type=F32)
            if bias_rows is not None:
                s = s + bias_ref[hh, bias_rows, :]
            m_cur = jnp.max(s, axis=0, keepdims=True)
            if init:
                m_next = m_cur
            else:
                m_prev = m_sc[j, c]
                m_next = jnp.maximum(m_prev, m_cur)
            p = jnp.exp2(s - m_next)
            p_sum = jnp.sum(p, axis=0, keepdims=True)
            pv = _dot(vt_ref[kt, hh * DIFF_V_DIM:(hh + 1) * DIFF_V_DIM, :], p.astype(BF16))
            if init:
                l_sc[j, c] = p_sum
                acc_sc[j, c] = pv
            else:
                alpha = jnp.exp2(m_prev - m_next)
                l_sc[j, c] = alpha * l_sc[j, c] + p_sum
                acc_sc[j, c] = acc_sc[j, c] * alpha + pv
            m_sc[j, c] = m_next

    for src, dst in zip(w_f32_refs, w_bf16_refs):
        dst[...] = src[...].astype(BF16)
    lam = (jnp.exp(jnp.sum(lq1_ref[...] * lk1_ref[...], axis=-1, keepdims=True))
           - jnp.exp(jnp.sum(lq2_ref[...] * lk2_ref[...], axis=-1, keepdims=True))
           + lam_init)
    for hh in range(ATTN_HEADS_PER_STEP):
        for j in range(n_q_tiles):
            key_tile(hh, j, j, slice(tq, 2 * tq), init=True)
            if j > 0:
                key_tile(hh, j, j - 1, slice(0, tq), init=False)
            for kt in range(j - 2, -1, -1):
                key_tile(hh, j, kt, None, init=False)
            o = (acc_sc[j, 2 * hh] * (1.0 / l_sc[j, 2 * hh])
                 - acc_sc[j, 2 * hh + 1] * (lam / l_sc[j, 2 * hh + 1]))
            r = lax.rsqrt(jnp.mean(o * o, axis=0, keepdims=True) + EPS) * (1.0 - lam_init)
            y = o * r * g_ref[...]
            o_ref[j * tq:(j + 1) * tq, hh * DIFF_V_DIM:(hh + 1) * DIFF_V_DIM] = y.T.astype(BF16)


def _attention(zqk, vt, bias_tile, lq1, lk1, lq2, lk2, subln_g_col, bsz, seq, lam_init,
               cast_weights):
    tq = ATTN_TQ
    assert seq % tq == 0
    nq = seq // tq
    hps = ATTN_HEADS_PER_STEP
    assert DIFF_HEADS % hps == 0
    nh = DIFF_HEADS // hps
    hw = hps * DIFF_V_DIM
    vec = pl.BlockSpec((1, DIFF_QK_DIM), lambda b, h: (0, 0))
    slab_specs = _slab_specs(cast_weights, bsz * nh, lambda b, h: (b * nh + h, 0))
    outs = pl.pallas_call(
        functools.partial(_attn_kernel, n_q_tiles=nq, n_cast=len(cast_weights),
                          lam_init=lam_init),
        grid=(bsz, nh),
        in_specs=[
            pl.BlockSpec((seq, hw), lambda b, h: (b, h)),
            pl.BlockSpec((seq, hw), lambda b, h: (b, nh + h)),
            pl.BlockSpec((None, nq, hw, tq), lambda b, h: (b, 0, h, 0)),
            pl.BlockSpec((hps, 2 * tq, tq), lambda b, h: (h, 0, 0)),
            vec, vec, vec, vec,
            pl.BlockSpec((DIFF_V_DIM, 1), lambda b, h: (0, 0)),
        ] + slab_specs,
        out_specs=[pl.BlockSpec((seq, hw), lambda b, h: (b, h))] + slab_specs,
        out_shape=[jax.ShapeDtypeStruct((bsz * seq, DIFF_WIDTH), BF16)]
        + [jax.ShapeDtypeStruct(w.shape, BF16) for w in cast_weights],
        scratch_shapes=[
            pltpu.VMEM((nq, 2 * hps, 1, tq), F32),
            pltpu.VMEM((nq, 2 * hps, 1, tq), F32),
            pltpu.VMEM((nq, 2 * hps, DIFF_V_DIM, tq), F32),
        ],
        compiler_params=_compiler_params(("arbitrary", "arbitrary")),
        name="diff_attention",
    )(zqk, zqk, vt, bias_tile, lq1, lk1, lq2, lk2, subln_g_col, *cast_weights)
    return outs[0], outs[1:]


def _mix_out_kernel(od_ref, gu_ref, gv_ref, x_ref, lng_ref, lnb_ref, ws_ref, bs_ref,
                    wo_ref, h1_ref, gm_sc):
    tm = od_ref.shape[0]
    h1_attn = x_ref[...] + _dot(od_ref[...], wo_ref[:DIFF_WIDTH, :])
    gv = gv_ref[...].astype(F32)
    mu = jnp.mean(gv, axis=-1, keepdims=True)
    xc = gv - mu
    vn = (xc * lax.rsqrt(jnp.mean(xc * xc, axis=-1, keepdims=True) + EPS)
          * lng_ref[...] + lnb_ref[...]).astype(BF16)

    row = lax.broadcasted_iota(jnp.int32, (GMLP_CHUNK, GMLP_CHUNK), 0)
    col = lax.broadcasted_iota(jnp.int32, (GMLP_CHUNK, GMLP_CHUNK), 1)
    causal = (col // CHUNK) <= (row // CHUNK)
    for g in range(GMLP_GROUPS):
        ws_g = jnp.where(causal, ws_ref[g], 0.0).astype(BF16)
        cols = slice(g * GMLP_GROUP_DIM, (g + 1) * GMLP_GROUP_DIM)
        for r in range(tm // GMLP_CHUNK):
            rows = slice(r * GMLP_CHUNK, (r + 1) * GMLP_CHUNK)
            s = _dot(ws_g, vn[rows, cols]) + bs_ref[g]
            gm_sc[rows, cols] = (gu_ref[rows, cols].astype(F32) * s).astype(BF16)

    h1_ref[...] = h1_attn + _dot(gm_sc[...], wo_ref[DIFF_WIDTH:, :])


def _mix_out(o_diff, zug, x2d, ln_g, ln_b, ws, bs, w_out_bf16):
    m, d = x2d.shape
    tm = MIX_TM
    assert m % tm == 0 and tm % GMLP_CHUNK == 0
    full = lambda shape: pl.BlockSpec(shape, lambda i: (0,) * len(shape))
    return pl.pallas_call(
        _mix_out_kernel,
        grid=(m // tm,),
        in_specs=[
            pl.BlockSpec((tm, DIFF_WIDTH), lambda i: (i, 0)),
            pl.BlockSpec((tm, GMLP_WIDTH), lambda i: (i, 0)),
            pl.BlockSpec((tm, GMLP_WIDTH), lambda i: (i, 1)),
            pl.BlockSpec((tm, d), lambda i: (i, 0)),
            full((1, GMLP_WIDTH)), full((1, GMLP_WIDTH)),
            full((GMLP_GROUPS, GMLP_CHUNK, GMLP_CHUNK)),
            full((GMLP_GROUPS, GMLP_CHUNK, 1)),
            full((DIFF_WIDTH + GMLP_WIDTH, d)),
        ],
        out_specs=pl.BlockSpec((tm, d), lambda i: (i, 0)),
        out_shape=jax.ShapeDtypeStruct((m, d), F32),
        scratch_shapes=[pltpu.VMEM((tm, GMLP_WIDTH), BF16)],
        compiler_params=_compiler_params(("arbitrary",)),
        name="mix_out",
    )(o_diff, zug, zug, x2d, ln_g, ln_b, ws, bs, w_out_bf16)


def _ffn_kernel(h1_ref, fg_ref, w1_ref, w3_ref, w2_ref, o_ref, f_sc):
    k = pl.program_id(1)

    def hidden_tile(f, base):
        act = (jax.nn.silu(_dot(f, w1_ref[...])) * _dot(f, w3_ref[...])).astype(BF16)
        o_ref[...] = base + _dot(act, w2_ref[...])

    @pl.when(k == 0)
    def _():
        h1 = h1_ref[...]
        f = (_rms_scale(h1) * fg_ref[...]).astype(BF16)
        f_sc[...] = f
        hidden_tile(f, h1)

    @pl.when(k > 0)
    def _():
        hidden_tile(f_sc[...], o_ref[...])


def _ffn(h1, ffn_g, w1, w3, w2):
    m, d = h1.shape
    hid = w1.shape[1]
    tm, th = FFN_TM, FFN_TH
    assert m % tm == 0 and hid % th == 0
    return pl.pallas_call(
        _ffn_kernel,
        grid=(m // tm, hid // th),
        in_specs=[
            pl.BlockSpec((tm, d), lambda i, k: (i, 0)),
            pl.BlockSpec((1, d), lambda i, k: (0, 0)),
            pl.BlockSpec((d, th), lambda i, k: (0, k)),
            pl.BlockSpec((d, th), lambda i, k: (0, k)),
            pl.BlockSpec((th, d), lambda i, k: (k, 0)),
        ],
        out_specs=pl.BlockSpec((tm, d), lambda i, k: (i, 0)),
        out_shape=jax.ShapeDtypeStruct((m, d), F32),
        scratch_shapes=[pltpu.VMEM((tm, d), BF16)],
        compiler_params=_compiler_params(("arbitrary", "arbitrary")),
        name="ffn",
    )(h1, ffn_g, w1, w3, w2)


def _ple_kernel(h2_ref, p_ref, wg_ref, wu_ref, g_ref, o_ref, *, final_norm):
    h2 = h2_ref[...]
    gate = _dot(h2.astype(BF16), wg_ref[...])
    up = _dot(p_ref[...].astype(BF16), wu_ref[...])
    h3 = h2 + up * jax.nn.sigmoid(gate)
    if final_norm:
        h3 = _rms_scale(h3) * g_ref[...]
    o_ref[...] = h3


def _ple(h2, p2d, w_gate, w_up, final_g, final_norm):
    m, d = h2.shape
    pd = p2d.shape[1]
    tm = PLE_TM
    assert m % tm == 0
    full = lambda shape: pl.BlockSpec(shape, lambda i: (0,) * len(shape))
    return pl.pallas_call(
        functools.partial(_ple_kernel, final_norm=final_norm),
        grid=(m // tm,),
        in_specs=[
            pl.BlockSpec((tm, d), lambda i: (i, 0)),
            pl.BlockSpec((tm, pd), lambda i: (i, 0)),
            full((d, d)), full((pd, d)), full((1, d)),
        ],
        out_specs=pl.BlockSpec((tm, d), lambda i: (i, 0)),
        out_shape=jax.ShapeDtypeStruct((m, d), F32),
        compiler_params=_compiler_params(("arbitrary",)),
        name="ple",
    )(h2, p2d, w_gate, w_up, final_g)


def kernel(x, p, w_in, w_out, attn_norm_g, ffn_norm_g, final_norm_g, lambda_q1, lambda_k1,
           lambda_q2, lambda_k2, subln_g, rel_bias, gmlp_ln_g, gmlp_ln_b, gmlp_ws, gmlp_b,
           ffn_w1, ffn_w3, ffn_w2, pl_w_up, pl_w_gate):
    bsz, seq, d = x.shape
    depth = w_in.shape[0]
    qk_cols = DIFF_HEADS * 2 * DIFF_QK_DIM
    assert w_in.shape[2] == 2 * qk_cols + DIFF_WIDTH + 2 * GMLP_WIDTH
    assert seq % GMLP_CHUNK == 0
    row = lambda v: v.reshape(1, -1)

    bias_tile = _bias_tile(rel_bias, seq)
    h = x.reshape(bsz * seq, d)
    for i in range(depth):
        lam_init = 0.8 - 0.6 * math.exp(-0.3 * i)
        zqk, vt, zug, (w2,) = _in_proj(h, row(attn_norm_g[i]), w_in[i], bsz, seq,
                                       cast_weights=(ffn_w2[i],))
        o_diff, (w1, w3, wo, wg) = _attention(
            zqk, vt, bias_tile, row(lambda_q1[i]), row(lambda_k1[i]), row(lambda_q2[i]),
            row(lambda_k2[i]), subln_g[i].reshape(-1, 1), bsz, seq, lam_init,
            cast_weights=(ffn_w1[i], ffn_w3[i], w_out[i], pl_w_gate[i]))
        h1 = _mix_out(o_diff, zug, h, row(gmlp_ln_g[i]), row(gmlp_ln_b[i]), gmlp_ws[i],
                      gmlp_b[i].reshape(GMLP_GROUPS, GMLP_CHUNK, 1), wo)
        h2 = _ffn(h1, row(ffn_norm_g[i]), w1, w3, w2)
        h = _ple(h2, p[i].reshape(bsz * seq, -1), wg, pl_w_up[i].astype(BF16),
                 row(final_norm_g), final_norm=(i == depth - 1))
    return h.reshape(bsz, seq, d)
```

```python
import functools
import math

import numpy as np
import jax
import jax.numpy as jnp
from jax import lax
from jax.experimental import pallas as pl
from jax.experimental.pallas import tpu as pltpu

F32 = jnp.float32
BF16 = jnp.bfloat16

CHUNK = 64
DIFF_HEADS = 4
DIFF_QK_DIM = 128
DIFF_V_DIM = 2 * DIFF_QK_DIM
DIFF_WIDTH = DIFF_HEADS * DIFF_V_DIM
GMLP_CHUNK = 128
GMLP_GROUPS = 8
GMLP_GROUP_DIM = 128
GMLP_WIDTH = GMLP_GROUPS * GMLP_GROUP_DIM
NUM_BUCKETS = 32
MAX_DISTANCE = 128
EPS = 1e-6
NEG_INF = -1e30
ATTN_SCALE = DIFF_QK_DIM ** -0.5
LOG2_E = math.log2(math.e)
Q_SCALE = ATTN_SCALE * LOG2_E

VMEM_LIMIT_BYTES = 56 * 1024 * 1024
BF16_SUBLANES = 16

IN_TM, IN_TN = 512, 1024
W_CHUNK, W_SLOTS = 128, 3
ATTN_TQ = 256
MIX_TM = 512
FFN_TM, FFN_TH = 1024, 512
PLE_TM = 512

_NT_DIMS = (((1,), (1,)), ((), ()))


def _dot(a, b):
    return jnp.dot(a, b, preferred_element_type=F32)


def _rms_scale(x):
    return x * lax.rsqrt(jnp.mean(x * x, axis=-1, keepdims=True) + EPS)


def _compiler_params(semantics):
    return pltpu.CompilerParams(dimension_semantics=semantics,
                                vmem_limit_bytes=VMEM_LIMIT_BYTES)


def _in_proj_kernel(x_ref, g_ref, w_hbm, *rest, n_cast):
    w_f32_refs = rest[:n_cast]
    qk_ref, vt_ref, ug_ref = rest[n_cast:n_cast + 3]
    w_bf16_refs = rest[n_cast + 3:2 * n_cast + 3]
    wqk_sc, wvt_sc, wug_sc, stage_sc, sem = rest[2 * n_cast + 3:]
    n_qk, n_v = wqk_sc.shape[1], wvt_sc.shape[0]

    n_slots = stage_sc.shape[0]

    def chunk_copy(c):
        slot = c % n_slots
        return pltpu.make_async_copy(w_hbm.at[c * W_CHUNK:(c + 1) * W_CHUNK, :],
                                     stage_sc.at[slot], sem.at[slot])

    @pl.when(pl.program_id(0) == 0)
    def _():
        n_chunks = w_hbm.shape[0] // W_CHUNK
        for c in range(n_slots - 1):
            chunk_copy(c).start()
        for c in range(n_chunks):
            if c + n_slots - 1 < n_chunks:
                chunk_copy(c + n_slots - 1).start()
            chunk_copy(c).wait()
            rows = slice(c * W_CHUNK, (c + 1) * W_CHUNK)
            blk = stage_sc[c % n_slots]
            wqk_sc[rows, :] = blk[:, :n_qk].astype(BF16)
            wvt_sc[:, rows] = blk[:, n_qk:n_qk + n_v].T.astype(BF16)
            wug_sc[rows, :] = blk[:, n_qk + n_v:].astype(BF16)

    for src, dst in zip(w_f32_refs, w_bf16_refs):
        dst[...] = src[...].astype(BF16)
    a = (_rms_scale(x_ref[...]) * g_ref[...]).astype(BF16)
    for c in range(0, n_qk, IN_TN):
        zc = _dot(a, wqk_sc[:, c:c + IN_TN])
        if c < DIFF_WIDTH:
            zc = zc * Q_SCALE
        qk_ref[:, c:c + IN_TN] = zc.astype(BF16)
    vt = lax.dot_general(wvt_sc[...], a, _NT_DIMS, preferred_element_type=F32).astype(BF16)
    for t in range(vt_ref.shape[0]):
        vt_ref[t] = vt[:, t * ATTN_TQ:(t + 1) * ATTN_TQ]
    for c in range(0, wug_sc.shape[1], IN_TN):
        ug_ref[:, c:c + IN_TN] = jax.nn.gelu(_dot(a, wug_sc[:, c:c + IN_TN])).astype(BF16)


def _slab_specs(weights, n_steps, index_map):
    specs = []
    for w in weights:
        rows, cols = w.shape
        assert rows % (n_steps * BF16_SUBLANES) == 0
        specs.append(pl.BlockSpec((rows // n_steps, cols), index_map))
    return specs


def _in_proj(x2d, g, w_in, bsz, seq, cast_weights):
    m, d = x2d.shape
    tm = IN_TM
    n_qk, n_v, n_ug = 2 * DIFF_WIDTH, DIFF_WIDTH, 2 * GMLP_WIDTH
    assert w_in.shape == (d, n_qk + n_v + n_ug) and d % W_CHUNK == 0 and d // W_CHUNK >= W_SLOTS
    assert seq % tm == 0 and tm % ATTN_TQ == 0
    assert n_qk % IN_TN == 0 and n_ug % IN_TN == 0 and DIFF_WIDTH % IN_TN == 0
    tiles_per_seq = seq // tm
    kt = tm // ATTN_TQ
    slab_specs = _slab_specs(cast_weights, m // tm, lambda i: (i, 0))
    outs = pl.pallas_call(
        functools.partial(_in_proj_kernel, n_cast=len(cast_weights)),
        grid=(m // tm,),
        in_specs=[
            pl.BlockSpec((tm, d), lambda i: (i, 0)),
            pl.BlockSpec((1, d), lambda i: (0, 0)),
            pl.BlockSpec(memory_space=pl.ANY),
        ] + slab_specs,
        out_specs=[
            pl.BlockSpec((tm, n_qk), lambda i: (i, 0)),
            pl.BlockSpec((None, kt, DIFF_WIDTH, ATTN_TQ),
                         lambda i: (i // tiles_per_seq, i % tiles_per_seq, 0, 0)),
            pl.BlockSpec((tm, n_ug), lambda i: (i, 0)),
        ] + slab_specs,
        out_shape=[
            jax.ShapeDtypeStruct((m, n_qk), BF16),
            jax.ShapeDtypeStruct((bsz, seq // ATTN_TQ, DIFF_WIDTH, ATTN_TQ), BF16),
            jax.ShapeDtypeStruct((m, n_ug), BF16),
        ] + [jax.ShapeDtypeStruct(w.shape, BF16) for w in cast_weights],
        scratch_shapes=[
            pltpu.VMEM((d, n_qk), BF16),
            pltpu.VMEM((n_v, d), BF16),
            pltpu.VMEM((d, n_ug), BF16),
            pltpu.VMEM((W_SLOTS, W_CHUNK, w_in.shape[1]), F32),
            pltpu.SemaphoreType.DMA((W_SLOTS,)),
        ],
        compiler_params=_compiler_params(("arbitrary",)),
        name="in_proj",
    )(x2d, g, w_in, *cast_weights)
    return outs[0], outs[1], outs[2], outs[3:]


def _t5_bucket_np(rel):
    half = NUM_BUCKETS // 2
    max_exact = half // 2
    ret = np.where(rel > 0, half, 0)
    n = np.abs(rel)
    nf = np.maximum(n, 1).astype(np.float32)
    large = max_exact + (np.log(nf / np.float32(max_exact))
                         / np.float32(math.log(MAX_DISTANCE / max_exact))
                         * np.float32(half - max_exact)).astype(np.int32)
    large = np.minimum(large, half - 1)
    return (ret + np.where(n < max_exact, n, large)).astype(np.int32)


def _near_bucket_table(seq):
    i = np.arange(ATTN_TQ)[None, :]
    j = np.arange(2 * ATTN_TQ)[:, None] - ATTN_TQ
    visible = np.floor_divide(j, CHUNK) <= i // CHUNK
    table = np.where(visible, _t5_bucket_np(j - i), -1).astype(np.int32)
    far = np.unique(_t5_bucket_np(-np.arange(ATTN_TQ + 1, max(seq, ATTN_TQ + 2))))
    assert far.size == 1, "far keys must share one bias bucket"
    return table, int(far[0])


def _bias_tile_kernel(bucket_ref, rb_ref, out_ref, *, buckets, far_bucket):
    bucket = bucket_ref[...]
    for h in range(DIFF_HEADS):
        shift = rb_ref[far_bucket, h]
        tile = jnp.full(bucket.shape, NEG_INF, F32)
        for b in buckets:
            tile = jnp.where(bucket == b, (rb_ref[b, h] - shift) * LOG2_E, tile)
        out_ref[h] = tile


def _bias_tile(rel_bias, seq):
    table, far_bucket = _near_bucket_table(seq)
    buckets = tuple(int(b) for b in np.unique(table) if b >= 0)
    return pl.pallas_call(
        functools.partial(_bias_tile_kernel, buckets=buckets, far_bucket=far_bucket),
        in_specs=[pl.BlockSpec(memory_space=pltpu.VMEM),
                  pl.BlockSpec(memory_space=pltpu.SMEM)],
        out_specs=pl.BlockSpec(memory_space=pltpu.VMEM),
        out_shape=jax.ShapeDtypeStruct((DIFF_HEADS,) + table.shape, F32),
        name="bias_tile",
    )(jnp.asarray(table), rel_bias)


def _attn_kernel(q_ref, k_ref, vt_ref, bias_ref, lq1_ref, lk1_ref, lq2_ref, lk2_ref,
                 g_ref, *rest, n_q_tiles, n_cast, lam_init):
    w_f32_refs = rest[:n_cast]
    o_ref = rest[n_cast]
    w_bf16_refs = rest[n_cast + 1:2 * n_cast + 1]
    m_sc, l_sc, acc_sc = rest[2 * n_cast + 1:]
    tq = ATTN_TQ

    def key_tile(j, kt, bias_rows, init):
        q_rows = slice(j * tq, (j + 1) * tq)
        for m in range(2):
            cols = slice(m * DIFF_QK_DIM, (m + 1) * DIFF_QK_DIM)
            s = lax.dot_general(k_ref[kt * tq:(kt + 1) * tq, cols], q_ref[q_rows, cols], _NT_DIMS,
                                preferred_element_type=F32)
            if bias_rows is not None:
                s = s + bias_ref[bias_rows, :]
            m_cur = jnp.max(s, axis=0, keepdims=True)
            if init:
                m_next = m_cur
            else:
                m_prev = m_sc[j, m]
                m_next = jnp.maximum(m_prev, m_cur)
            p = jnp.exp2(s - m_next)
            p_sum = jnp.sum(p, axis=0, keepdims=True)
            pv = _dot(vt_ref[kt], p.astype(BF16))
            if init:
                l_sc[j, m] = p_sum
                acc_sc[j, m] = pv
            else:
                alpha = jnp.exp2(m_prev - m_next)
                l_sc[j, m] = alpha * l_sc[j, m] + p_sum
                acc_sc[j, m] = acc_sc[j, m] * alpha + pv
            m_sc[j, m] = m_next

    for src, dst in zip(w_f32_refs, w_bf16_refs):
        dst[...] = src[...].astype(BF16)
    lam = (jnp.exp(jnp.sum(lq1_ref[...] * lk1_ref[...], axis=-1, keepdims=True))
           - jnp.exp(jnp.sum(lq2_ref[...] * lk2_ref[...], axis=-1, keepdims=True))
           + lam_init)
    for j in range(n_q_tiles):
        key_tile(j, j, slice(tq, 2 * tq), init=True)
        if j > 0:
            key_tile(j, j - 1, slice(0, tq), init=False)
        for kt in range(j - 2, -1, -1):
            key_tile(j, kt, None, init=False)
        o = acc_sc[j, 0] * (1.0 / l_sc[j, 0]) - acc_sc[j, 1] * (lam / l_sc[j, 1])
        r = lax.rsqrt(jnp.mean(o * o, axis=0, keepdims=True) + EPS) * (1.0 - lam_init)
        y = o * r * g_ref[...]
        o_ref[j * tq:(j + 1) * tq, :] = y.T.astype(BF16)


def _attention(zqk, vt, bias_tile, lq1, lk1, lq2, lk2, subln_g_col, bsz, seq, lam_init,
               cast_weights):
    tq = ATTN_TQ
    assert seq % tq == 0
    nq = seq // tq
    nh = DIFF_HEADS
    vec = pl.BlockSpec((1, DIFF_QK_DIM), lambda b, h: (0, 0))
    slab_specs = _slab_specs(cast_weights, bsz * nh, lambda b, h: (b * nh + h, 0))
    outs = pl.pallas_call(
        functools.partial(_attn_kernel, n_q_tiles=nq, n_cast=len(cast_weights),
                          lam_init=lam_init),
        grid=(bsz, nh),
        in_specs=[
            pl.BlockSpec((seq, DIFF_V_DIM), lambda b, h: (b, h)),
            pl.BlockSpec((seq, DIFF_V_DIM), lambda b, h: (b, nh + h)),
            pl.BlockSpec((None, nq, DIFF_V_DIM, tq), lambda b, h: (b, 0, h, 0)),
            pl.BlockSpec((None, 2 * tq, tq), lambda b, h: (h, 0, 0)),
            vec, vec, vec, vec,
            pl.BlockSpec((DIFF_V_DIM, 1), lambda b, h: (0, 0)),
        ] + slab_specs,
        out_specs=[pl.BlockSpec((seq, DIFF_V_DIM), lambda b, h: (b, h))] + slab_specs,
        out_shape=[jax.ShapeDtypeStruct((bsz * seq, DIFF_WIDTH), BF16)]
        + [jax.ShapeDtypeStruct(w.shape, BF16) for w in cast_weights],
        scratch_shapes=[
            pltpu.VMEM((nq, 2, 1, tq), F32),
            pltpu.VMEM((nq, 2, 1, tq), F32),
            pltpu.VMEM((nq, 2, DIFF_V_DIM, tq), F32),
        ],
        compiler_params=_compiler_params(("arbitrary", "arbitrary")),
        name="diff_attention",
    )(zqk, zqk, vt, bias_tile, lq1, lk1, lq2, lk2, subln_g_col, *cast_weights)
    return outs[0], outs[1:]


def _mix_out_kernel(od_ref, gu_ref, gv_ref, x_ref, lng_ref, lnb_ref, ws_ref, bs_ref,
                    wo_ref, h1_ref, gm_sc):
    tm = od_ref.shape[0]
    h1_attn = x_ref[...] + _dot(od_ref[...], wo_ref[:DIFF_WIDTH, :])
    gv = gv_ref[...].astype(F32)
    mu = jnp.mean(gv, axis=-1, keepdims=True)
    xc = gv - mu
    vn = (xc * lax.rsqrt(jnp.mean(xc * xc, axis=-1, keepdims=True) + EPS)
          * lng_ref[...] + lnb_ref[...]).astype(BF16)

    row = lax.broadcasted_iota(jnp.int32, (GMLP_CHUNK, GMLP_CHUNK), 0)
    col = lax.broadcasted_iota(jnp.int32, (GMLP_CHUNK, GMLP_CHUNK), 1)
    causal = (col // CHUNK) <= (row // CHUNK)
    for g in range(GMLP_GROUPS):
        ws_g = jnp.where(causal, ws_ref[g], 0.0).astype(BF16)
        cols = slice(g * GMLP_GROUP_DIM, (g + 1) * GMLP_GROUP_DIM)
        for r in range(tm // GMLP_CHUNK):
            rows = slice(r * GMLP_CHUNK, (r + 1) * GMLP_CHUNK)
            s = _dot(ws_g, vn[rows, cols]) + bs_ref[g]
            gm_sc[rows, cols] = (gu_ref[rows, cols].astype(F32) * s).astype(BF16)

    h1_ref[...] = h1_attn + _dot(gm_sc[...], wo_ref[DIFF_WIDTH:, :])


def _mix_out(o_diff, zug, x2d, ln_g, ln_b, ws, bs, w_out_bf16):
    m, d = x2d.shape
    tm = MIX_TM
    assert m % tm == 0 and tm % GMLP_CHUNK == 0
    full = lambda shape: pl.BlockSpec(shape, lambda i: (0,) * len(shape))
    return pl.pallas_call(
        _mix_out_kernel,
        grid=(m // tm,),
        in_specs=[
            pl.BlockSpec((tm, DIFF_WIDTH), lambda i: (i, 0)),
            pl.BlockSpec((tm, GMLP_WIDTH), lambda i: (i, 0)),
            pl.BlockSpec((tm, GMLP_WIDTH), lambda i: (i, 1)),
            pl.BlockSpec((tm, d), lambda i: (i, 0)),
            full((1, GMLP_WIDTH)), full((1, GMLP_WIDTH)),
            full((GMLP_GROUPS, GMLP_CHUNK, GMLP_CHUNK)),
            full((GMLP_GROUPS, GMLP_CHUNK, 1)),
            full((DIFF_WIDTH + GMLP_WIDTH, d)),
        ],
        out_specs=pl.BlockSpec((tm, d), lambda i: (i, 0)),
        out_shape=jax.ShapeDtypeStruct((m, d), F32),
        scratch_shapes=[pltpu.VMEM((tm, GMLP_WIDTH), BF16)],
        compiler_params=_compiler_params(("arbitrary",)),
        name="mix_out",
    )(o_diff, zug, zug, x2d, ln_g, ln_b, ws, bs, w_out_bf16)


def _ffn_kernel(h1_ref, fg_ref, w1_ref, w3_ref, w2_ref, o_ref, f_sc):
    k = pl.program_id(1)

    def hidden_tile(f, base):
        half = w1_ref.shape[1] // 2
        for c in (0, half):
            act = (jax.nn.silu(_dot(f, w1_ref[:, c:c + half]))
                   * _dot(f, w3_ref[:, c:c + half])).astype(BF16)
            prev = base if c == 0 else o_ref[...]
            o_ref[...] = prev + _dot(act, w2_ref[c:c + half, :])

    @pl.when(k == 0)
    def _():
        h1 = h1_ref[...]
        f = (_rms_scale(h1) * fg_ref[...]).astype(BF16)
        f_sc[...] = f
        hidden_tile(f, h1)

    @pl.when(k > 0)
    def _():
        hidden_tile(f_sc[...], o_ref[...])


def _ffn(h1, ffn_g, w1, w3, w2):
    m, d = h1.shape
    hid = w1.shape[1]
    tm, th = FFN_TM, FFN_TH
    assert m % tm == 0 and hid % th == 0
    return pl.pallas_call(
        _ffn_kernel,
        grid=(m // tm, hid // th),
        in_specs=[
            pl.BlockSpec((tm, d), lambda i, k: (i, 0)),
            pl.BlockSpec((1, d), lambda i, k: (0, 0)),
            pl.BlockSpec((d, th), lambda i, k: (0, k)),
            pl.BlockSpec((d, th), lambda i, k: (0, k)),
            pl.BlockSpec((th, d), lambda i, k: (k, 0)),
        ],
        out_specs=pl.BlockSpec((tm, d), lambda i, k: (i, 0)),
        out_shape=jax.ShapeDtypeStruct((m, d), F32),
        scratch_shapes=[pltpu.VMEM((tm, d), BF16)],
        compiler_params=_compiler_params(("arbitrary", "arbitrary")),
        name="ffn",
    )(h1, ffn_g, w1, w3, w2)


def _ple_kernel(h2_ref, p_ref, wg_ref, wu_ref, g_ref, o_ref, *, final_norm):
    h2 = h2_ref[...]
    gate = _dot(h2.astype(BF16), wg_ref[...])
    up = _dot(p_ref[...].astype(BF16), wu_ref[...])
    h3 = h2 + up * jax.nn.sigmoid(gate)
    if final_norm:
        h3 = _rms_scale(h3) * g_ref[...]
    o_ref[...] = h3


def _ple(h2, p2d, w_gate, w_up, final_g, final_norm):
    m, d = h2.shape
    pd = p2d.shape[1]
    tm = PLE_TM
    assert m % tm == 0
    full = lambda shape: pl.BlockSpec(shape, lambda i: (0,) * len(shape))
    return pl.pallas_call(
        functools.partial(_ple_kernel, final_norm=final_norm),
        grid=(m // tm,),
        in_specs=[
            pl.BlockSpec((tm, d), lambda i: (i, 0)),
            pl.BlockSpec((tm, pd), lambda i: (i, 0)),
            full((d, d)), full((pd, d)), full((1, d)),
        ],
        out_specs=pl.BlockSpec((tm, d), lambda i: (i, 0)),
        out_shape=jax.ShapeDtypeStruct((m, d), F32),
        compiler_params=_compiler_params(("arbitrary",)),
        name="ple",
    )(h2, p2d, w_gate, w_up, final_g)


def kernel(x, p, w_in, w_out, attn_norm_g, ffn_norm_g, final_norm_g, lambda_q1, lambda_k1,
           lambda_q2, lambda_k2, subln_g, rel_bias, gmlp_ln_g, gmlp_ln_b, gmlp_ws, gmlp_b,
           ffn_w1, ffn_w3, ffn_w2, pl_w_up, pl_w_gate):
    bsz, seq, d = x.shape
    depth = w_in.shape[0]
    qk_cols = DIFF_HEADS * 2 * DIFF_QK_DIM
    assert w_in.shape[2] == 2 * qk_cols + DIFF_WIDTH + 2 * GMLP_WIDTH
    assert seq % GMLP_CHUNK == 0
    row = lambda v: v.reshape(1, -1)

    bias_tile = _bias_tile(rel_bias, seq)
    h = x.reshape(bsz * seq, d)
    for i in range(depth):
        lam_init = 0.8 - 0.6 * math.exp(-0.3 * i)
        zqk, vt, zug, (w2,) = _in_proj(h, row(attn_norm_g[i]), w_in[i], bsz, seq,
                                       cast_weights=(ffn_w2[i],))
        o_diff, (w1, w3, wo, wg) = _attention(
            zqk, vt, bias_tile, row(lambda_q1[i]), row(lambda_k1[i]), row(lambda_q2[i]),
            row(lambda_k2[i]), subln_g[i].reshape(-1, 1), bsz, seq, lam_init,
            cast_weights=(ffn_w1[i], ffn_w3[i], w_out[i], pl_w_gate[i]))
        h1 = _mix_out(o_diff, zug, h, row(gmlp_ln_g[i]), row(gmlp_ln_b[i]), gmlp_ws[i],
                      gmlp_b[i].reshape(GMLP_GROUPS, GMLP_CHUNK, 1), wo)
        h2 = _ffn(h1, row(ffn_norm_g[i]), w1, w3, w2)
        h = _ple(h2, p[i].reshape(bsz * seq, -1), wg, pl_w_up[i].astype(BF16),
                 row(final_norm_g), final_norm=(i == depth - 1))
    return h.reshape(bsz, seq, d)
```
